```python
import math
import jax, jax.numpy as jnp
from jax import lax
import numpy as np

D_MODEL = 1024
BATCH = 8
SEQ = 4096
DEPTH = 2

CHUNK = 64
NORM_EPS = 1e-6
S5_WIDTH = D_MODEL // 2
S5_GROUP = 16
S5_GROUPS = S5_WIDTH // S5_GROUP
S5_STATE = 64
CONV_WIDTH = D_MODEL // 2
CONV_KERNEL = 31
EVEN_IN = 2 * S5_WIDTH + 3 * CONV_WIDTH
N_HEADS = 16
HEAD_DIM = D_MODEL // N_HEADS
N_KV_HEADS = 4
Q_PER_KV = N_HEADS // N_KV_HEADS
ATT_WIDTH = N_HEADS * HEAD_DIM
KV_WIDTH = N_KV_HEADS * HEAD_DIM
IDX_HEADS = 8
IDX_DIM = 64
TOPK_MAX = 256
ODD_IN = 2 * ATT_WIDTH + 2 * KV_WIDTH + IDX_HEADS * IDX_DIM + IDX_DIM + IDX_HEADS
ROPE_THETA = 500000.0
ROT_DIM = HEAD_DIM // 4
N_EVEN = (DEPTH + 1) // 2
N_ODD = DEPTH // 2

kernel_name = "chunk_causal_s5_conformer_dsa_hybrid"


def rms_norm(x, g):
    xf = x.astype(jnp.float32)
    y = xf * lax.rsqrt(jnp.mean(xf * xf, axis=-1, keepdims=True) + NORM_EPS)
    return (y * g.astype(jnp.float32)).astype(x.dtype)


def rope_tables(length):
    pos = jnp.arange(length, dtype=jnp.float32)
    inv = ROPE_THETA ** (-jnp.arange(0, ROT_DIM, 2, dtype=jnp.float32) / ROT_DIM)
    ang = pos[:, None] * inv[None, :]
    return jnp.cos(ang), jnp.sin(ang)


def partial_rope(x, cos, sin):
    half = ROT_DIM // 2
    x1, x2, rest = x[..., :half], x[..., half:ROT_DIM], x[..., ROT_DIM:]
    shape = (1, cos.shape[0]) + (1,) * (x.ndim - 3) + (half,)
    c = cos.reshape(shape).astype(x.dtype)
    s = sin.reshape(shape).astype(x.dtype)
    return jnp.concatenate([x1 * c - x2 * s, x1 * s + x2 * c, rest], axis=-1)


def s5_mixer(u, lam_re, lam_im, log_step, b_re, b_im, c_re, c_im, d_skip, w_glu, b_glu):
    f32 = jnp.float32
    bsz, length, _ = u.shape
    uf = u.astype(f32).reshape(bsz, length, S5_GROUPS, S5_GROUP)
    lr = jnp.minimum(lam_re.astype(f32), -1e-4)
    li = lam_im.astype(f32)
    dt = jnp.exp(log_step.astype(f32))[:, None]
    mag = jnp.exp(lr * dt)
    ab_re = mag * jnp.cos(li * dt)
    ab_im = mag * jnp.sin(li * dt)
    den = lr * lr + li * li
    nr = ab_re - 1.0
    ni = ab_im
    k_re = (nr * lr + ni * li) / den
    k_im = (ni * lr - nr * li) / den
    br = b_re.astype(f32)
    bi = b_im.astype(f32)
    bb_re = k_re[..., None] * br - k_im[..., None] * bi
    bb_im = k_re[..., None] * bi + k_im[..., None] * br
    x_re = jnp.einsum("blgc,gnc->blgn", uf, bb_re)
    x_im = jnp.einsum("blgc,gnc->blgn", uf, bb_im)
    a_re = jnp.broadcast_to(ab_re[None, None], (1, length, S5_GROUPS, S5_STATE))
    a_im = jnp.broadcast_to(ab_im[None, None], (1, length, S5_GROUPS, S5_STATE))

    def combine(e1, e2):
        a1r, a1i, b1r, b1i = e1
        a2r, a2i, b2r, b2i = e2
        return (a2r * a1r - a2i * a1i,
                a2r * a1i + a2i * a1r,
                a2r * b1r - a2i * b1i + b2r,
                a2r * b1i + a2i * b1r + b2i)

    _, _, s_re, s_im = lax.associative_scan(combine, (a_re, a_im, x_re, x_im), axis=1)
    y = (jnp.einsum("blgn,gcn->blgc", s_re, c_re.astype(f32))
         - jnp.einsum("blgn,gcn->blgc", s_im, c_im.astype(f32)))
    y = y + uf * d_skip.astype(f32).reshape(S5_GROUPS, S5_GROUP)
    g = jax.nn.gelu(y.reshape(bsz, length, S5_WIDTH))
    out = g * jax.nn.sigmoid(g @ w_glu.astype(f32) + b_glu.astype(f32))
    return out.astype(u.dtype)


def conformer_conv(v, g, conv_w, conv_b, ln_g, ln_b):
    h = v * jax.nn.sigmoid(g)
    h = lax.conv_general_dilated(
        h, conv_w[:, None, :], window_strides=(1,), padding=[(CONV_KERNEL - 1, 0)],
        dimension_numbers=("NWC", "WIO", "NWC"), feature_group_count=CONV_WIDTH) + conv_b
    hf = h.astype(jnp.float32)
    mu = jnp.mean(hf, axis=-1, keepdims=True)
    var = jnp.mean(jnp.square(hf - mu), axis=-1, keepdims=True)
    hf = (hf - mu) * lax.rsqrt(var + NORM_EPS) * ln_g.astype(jnp.float32) + ln_b.astype(jnp.float32)
    return jax.nn.silu(hf).astype(v.dtype)


def even_layer(h, w_in, lam_re, lam_im, log_step, b_re, b_im, c_re, c_im, d_skip, w_glu, b_glu,
               conv_w, conv_b, ln_g, ln_b, w_out):
    proj = h @ w_in
    u_a = proj[..., :S5_WIDTH]
    z_a = proj[..., S5_WIDTH:2 * S5_WIDTH]
    o = 2 * S5_WIDTH
    v_b = proj[..., o:o + CONV_WIDTH]
    g_b = proj[..., o + CONV_WIDTH:o + 2 * CONV_WIDTH]
    z_b = proj[..., o + 2 * CONV_WIDTH:]
    y_a = s5_mixer(u_a, lam_re, lam_im, log_step, b_re, b_im, c_re, c_im, d_skip, w_glu, b_glu)
    y_b = conformer_conv(v_b, g_b, conv_w, conv_b, ln_g, ln_b)
    y = jnp.concatenate([y_a * jax.nn.silu(z_a), y_b * jax.nn.silu(z_b)], axis=-1)
    return y @ w_out


def odd_layer(h, w_in, w_out):
    bsz, length, _ = h.shape
    f32 = jnp.float32
    proj = h @ w_in
    sizes = [ATT_WIDTH, ATT_WIDTH, KV_WIDTH, KV_WIDTH, IDX_HEADS * IDX_DIM, IDX_DIM]
    offs = np.cumsum([0] + sizes).tolist()
    q, z, k, v, qi, ki = [proj[..., offs[i]:offs[i + 1]] for i in range(len(sizes))]
    wi = proj[..., offs[-1]:]
    cos, sin = rope_tables(length)
    q = partial_rope(q.reshape(bsz, length, N_KV_HEADS, Q_PER_KV, HEAD_DIM), cos, sin)
    k = partial_rope(k.reshape(bsz, length, N_KV_HEADS, HEAD_DIM), cos, sin)
    v = v.reshape(bsz, length, N_KV_HEADS, HEAD_DIM)
    qi = partial_rope(qi.reshape(bsz, length, IDX_HEADS, IDX_DIM), cos, sin).astype(f32)
    ki = partial_rope(ki, cos, sin).astype(f32)
    wi = wi.astype(f32) * (IDX_HEADS ** -0.5) * (IDX_DIM ** -0.5)
    n_blk = length // CHUNK
    ksel = min(TOPK_MAX, length // 4)
    key_chunk = jnp.arange(length) // CHUNK
    scale = HEAD_DIM ** -0.5

    def block(c):
        s0 = c * CHUNK
        qb = lax.dynamic_slice_in_dim(q, s0, CHUNK, axis=1)
        qib = lax.dynamic_slice_in_dim(qi, s0, CHUNK, axis=1)
        wib = lax.dynamic_slice_in_dim(wi, s0, CHUNK, axis=1)
        isc = jax.nn.relu(jnp.einsum("bqhd,bsd->bqhs", qib, ki))
        score = jnp.einsum("bqh,bqhs->bqs", wib, isc)
        score = jnp.where((key_chunk <= c)[None, None, :], score, -jnp.inf)
        _, idx = lax.top_k(score, ksel)
        valid = (idx // CHUNK) <= c
        kg = jax.vmap(lambda kk, ii: kk[ii])(k, idx)
        vg = jax.vmap(lambda vv, ii: vv[ii])(v, idx)
        logits = jnp.einsum("bqhgd,bqkhd->bqhgk", qb.astype(f32), kg.astype(f32)) * scale
        logits = jnp.where(valid[:, :, None, None, :], logits, -jnp.inf)
        p = jax.nn.softmax(logits, axis=-1).astype(v.dtype)
        ob = jnp.einsum("bqhgk,bqkhd->bqhgd", p, vg)
        return ob.reshape(bsz, CHUNK, ATT_WIDTH)

    out = lax.map(block, jnp.arange(n_blk))
    out = out.transpose(1, 0, 2, 3).reshape(bsz, length, ATT_WIDTH)
    return (out * jax.nn.silu(z)) @ w_out


def setup_inputs(seed: int = 0) -> dict:
    key = jax.random.key(seed)
    ks = jax.random.split(key, 24)
    f32 = jnp.float32
    nrm = lambda k, shape, s: jax.random.normal(k, shape, f32) * s
    G, N, C = S5_GROUPS, S5_STATE, S5_GROUP
    n_idx = jnp.arange(N, dtype=f32)
    return {
        "x": nrm(ks[0], (BATCH, SEQ, D_MODEL), 1.0),
        "norm_g": 1.0 + nrm(ks[1], (DEPTH, D_MODEL), 0.02),
        "e_w_in": nrm(ks[2], (N_EVEN, D_MODEL, EVEN_IN), D_MODEL ** -0.5),
        "e_lam_re": -0.5 + nrm(ks[3], (N_EVEN, G, N), 0.01),
        "e_lam_im": jnp.pi * n_idx[None, None, :] + nrm(ks[4], (N_EVEN, G, N), 0.01),
        "e_log_step": jax.random.uniform(ks[5], (N_EVEN, G), f32, math.log(1e-3), math.log(1e-1)),
        "e_b_re": nrm(ks[6], (N_EVEN, G, N, C), (2 * C) ** -0.5),
        "e_b_im": nrm(ks[7], (N_EVEN, G, N, C), (2 * C) ** -0.5),
        "e_c_re": nrm(ks[8], (N_EVEN, G, C, N), N ** -0.5),
        "e_c_im": nrm(ks[9], (N_EVEN, G, C, N), N ** -0.5),
        "e_d_skip": nrm(ks[10], (N_EVEN, S5_WIDTH), 1.0),
        "e_w_glu": nrm(ks[11], (N_EVEN, S5_WIDTH, S5_WIDTH), S5_WIDTH ** -0.5),
        "e_b_glu": nrm(ks[12], (N_EVEN, S5_WIDTH), 0.01),
        "e_conv_w": nrm(ks[13], (N_EVEN, CONV_KERNEL, CONV_WIDTH), CONV_KERNEL ** -0.5),
        "e_conv_b": nrm(ks[14], (N_EVEN, CONV_WIDTH), 0.01),
        "e_ln_g": 1.0 + nrm(ks[15], (N_EVEN, CONV_WIDTH), 0.02),
        "e_ln_b": nrm(ks[16], (N_EVEN, CONV_WIDTH), 0.01),
        "e_w_out": nrm(ks[17], (N_EVEN, S5_WIDTH + CONV_WIDTH, D_MODEL), (S5_WIDTH + CONV_WIDTH) ** -0.5),
        "o_w_in": nrm(ks[18], (N_ODD, D_MODEL, ODD_IN), D_MODEL ** -0.5),
        "o_w_out": nrm(ks[19], (N_ODD, ATT_WIDTH, D_MODEL), ATT_WIDTH ** -0.5),
        "final_g": 1.0 + nrm(ks[20], (D_MODEL,), 0.02),
    }


def reference(x, norm_g, e_w_in, e_lam_re, e_lam_im, e_log_step, e_b_re, e_b_im, e_c_re, e_c_im,
              e_d_skip, e_w_glu, e_b_glu, e_conv_w, e_conv_b, e_ln_g, e_ln_b, e_w_out,
              o_w_in, o_w_out, final_g):
    h = x
    for layer in range(DEPTH):
        hn = rms_norm(h, norm_g[layer])
        if layer % 2 == 0:
            j = layer // 2
            y = even_layer(hn, e_w_in[j], e_lam_re[j], e_lam_im[j], e_log_step[j], e_b_re[j], e_b_im[j],
                           e_c_re[j], e_c_im[j], e_d_skip[j], e_w_glu[j], e_b_glu[j], e_conv_w[j],
                           e_conv_b[j], e_ln_g[j], e_ln_b[j], e_w_out[j])
        else:
            j = layer // 2
            y = odd_layer(hn, o_w_in[j], o_w_out[j])
        h = h + y.astype(h.dtype)
    return rms_norm(h, final_g)
```

```python
import functools
import math

import numpy as np
import jax
import jax.numpy as jnp
from jax import lax
from jax.experimental import pallas as pl
from jax.experimental.pallas import tpu as pltpu

F32 = jnp.float32
BF16 = jnp.bfloat16
I32 = jnp.int32

NBATCH = 8
D_MODEL = 1024
CHUNK = 64
NORM_EPS = 1e-6
S5_WIDTH = 512
S5_GROUP = 16
S5_GROUPS = 32
S5_STATE = 64
S5_NSTATE = S5_GROUPS * S5_STATE
S5_SLABS = 4
CONV_WIDTH = 512
CONV_KERNEL = 31
CONV_HALO = (CONV_KERNEL - 1) * NBATCH
EVEN_IN = 2 * S5_WIDTH + 3 * CONV_WIDTH
N_HEADS = 16
HEAD_DIM = 64
N_KV_HEADS = 4
Q_PER_KV = 4
ATT_WIDTH = 1024
KV_WIDTH = 256
IDX_HEADS = 8
IDX_DIM = 64
TOPK_MAX = 256
ROPE_THETA = 500000.0
ROT_DIM = 16
LANES = 128
ODD_PAD = 3200
ATT_TILE = 256
INT_MIN = np.int32(-2 ** 31)
F32_MAX = float(np.finfo(np.float32).max)
NEG_BIG = -1e38
LOG2E = math.log2(math.e)

VMEM_LIMIT = 56 * 1024 * 1024


def _dot(a, b):
    return jnp.dot(a, b, preferred_element_type=F32)


def _dot_nt(a, b):
    return lax.dot_general(a, b, (((1,), (1,)), ((), ())), preferred_element_type=F32)


def _sigmoid(x):
    return 1.0 / (1.0 + jnp.exp(-x))


def _silu(x):
    return x * _sigmoid(x)


def _gelu_tanh(x):
    c = math.sqrt(2.0 / math.pi)
    return 0.5 * x * (1.0 + jnp.tanh(c * (x + 0.044715 * (x * x * x))))


def _s5_prep_kernel(lr_ref, li_ref, ls_ref, br_ref, bi_ref, abr_ref, abi_ref, bbr_ref, bbi_ref):
    lr = jnp.minimum(lr_ref[...], -1e-4)
    li = li_ref[...]
    dt = jnp.exp(ls_ref[...])
    mag = jnp.exp(lr * dt)
    ab_re = mag * jnp.cos(li * dt)
    ab_im = mag * jnp.sin(li * dt)
    den = lr * lr + li * li
    nr = ab_re - 1.0
    ni = ab_im
    k_re = (nr * lr + ni * li) / den
    k_im = (ni * lr - nr * li) / den
    br = br_ref[...]
    bi = bi_ref[...]
    abr_ref[...] = ab_re
    abi_ref[...] = ab_im
    bbr_ref[...] = k_re * br - k_im * bi
    bbi_ref[...] = k_re * bi + k_im * br


def _s5_prep(lam_re, lam_im, log_step, b_re, b_im):
    g, n, c = b_re.shape
    shape2 = (g * n * c // LANES, LANES)
    expand = lambda a: jnp.broadcast_to(a[:, :, None], (g, n, c)).reshape(shape2)
    ls = jnp.broadcast_to(log_step[:, None], (g, n))
    outs = pl.pallas_call(
        _s5_prep_kernel,
        out_shape=[jax.ShapeDtypeStruct(shape2, F32)] * 4,
        name="s5_prep",
    )(expand(lam_re), expand(lam_im), expand(ls), b_re.reshape(shape2), b_im.reshape(shape2))
    ab_re, ab_im, bb_re, bb_im = [o.reshape(g, n, c) for o in outs]
    return ab_re[:, :, 0], ab_im[:, :, 0], bb_re, bb_im


def _s5_block_weights(bb_re, bb_im, c_re, c_im):
    gl = S5_GROUPS // S5_SLABS
    eye = jnp.eye(gl, dtype=F32)

    def in_map(bb):
        t = bb.reshape(S5_SLABS, gl, S5_STATE, S5_GROUP).transpose(0, 1, 3, 2)
        t = t[:, :, :, None, :] * eye[None, :, None, :, None]
        return t.reshape(S5_SLABS, gl * S5_GROUP, gl * S5_STATE)

    def out_map(cc):
        t = cc.reshape(S5_SLABS, gl, S5_GROUP, S5_STATE).transpose(0, 1, 3, 2)
        t = t[:, :, :, None, :] * eye[None, :, None, :, None]
        return t.reshape(S5_SLABS, gl * S5_STATE, gl * S5_GROUP)

    bc = jnp.concatenate([in_map(bb_re), in_map(bb_im)], axis=2).astype(BF16)
    cc = jnp.concatenate([out_map(c_re), out_map(-c_im)], axis=1).astype(BF16)
    return bc, cc


def _layer0_kernel(x_ref, g_ref, win_ref, are_ref, aim_ref, bc_ref, cc_ref, dsk_ref, wglu_ref, bglu_ref,
                   cw_ref, cb_ref, lng_ref, lnb_ref, wout_ref, o_ref,
                   hn_ref, proj_ref, xs_ref, st_ref, hc_ref, gl_ref, ycat_ref, *, tt):
    rows = NBATCH * tt
    half = S5_NSTATE
    slab_w = half // S5_SLABS

    @pl.when(pl.program_id(0) == 0)
    def _():
        st_ref[...] = jnp.zeros_like(st_ref)
        hc_ref[0:CONV_HALO, :] = jnp.zeros((CONV_HALO, CONV_WIDTH), F32)

    gain = g_ref[...]
    for tp in range(tt // 2):
        xa = x_ref[:, (2 * tp) * D_MODEL:(2 * tp + 1) * D_MODEL]
        xb = x_ref[:, (2 * tp + 1) * D_MODEL:(2 * tp + 2) * D_MODEL]
        xx = jnp.concatenate([xa, xb], axis=0)
        ms = jnp.mean(xx * xx, axis=-1, keepdims=True)
        hn_ref[16 * tp:16 * tp + 16, :] = (xx * lax.rsqrt(ms + NORM_EPS) * gain).astype(BF16)

    proj_ref[...] = _dot(hn_ref[...], win_ref[...])

    for m in range(S5_SLABS):
        u_m = proj_ref[:, LANES * m:LANES * (m + 1)].astype(BF16)
        xm = _dot(u_m, bc_ref[m])
        xs_ref[:, slab_w * m:slab_w * (m + 1)] = xm[:, :slab_w]
        xs_ref[:, half + slab_w * m:half + slab_w * (m + 1)] = xm[:, slab_w:]

    cw = 512
    for cc_i in range(half // cw):
        lo = cc_i * cw
        a_re = are_ref[:, lo:lo + cw]
        a_im = aim_ref[:, lo:lo + cw]

        def step(t, carry, lo=lo, a_re=a_re, a_im=a_im):
            s_re, s_im = carry
            r0 = pl.multiple_of(t * NBATCH, NBATCH)
            x_re = xs_ref[pl.ds(r0, NBATCH), lo:lo + cw]
            x_im = xs_ref[pl.ds(r0, NBATCH), half + lo:half + lo + cw]
            n_re = a_re * s_re - a_im * s_im + x_re
            n_im = a_re * s_im + a_im * s_re + x_im
            xs_ref[pl.ds(r0, NBATCH), lo:lo + cw] = n_re
            xs_ref[pl.ds(r0, NBATCH), half + lo:half + lo + cw] = n_im
            return n_re, n_im

        s_re, s_im = lax.fori_loop(0, tt, step, (st_ref[:, lo:lo + cw], st_ref[:, half + lo:half + lo + cw]),
                                   unroll=8)
        st_ref[:, lo:lo + cw] = s_re
        st_ref[:, half + lo:half + lo + cw] = s_im

    for m in range(S5_SLABS):
        s_re = xs_ref[:, slab_w * m:slab_w * (m + 1)].astype(BF16)
        s_im = xs_ref[:, half + slab_w * m:half + slab_w * (m + 1)].astype(BF16)
        y_m = _dot(s_re, cc_ref[m, 0:slab_w, :]) + _dot(s_im, cc_ref[m, slab_w:2 * slab_w, :])
        y_m = y_m + proj_ref[:, LANES * m:LANES * (m + 1)] * dsk_ref[:, LANES * m:LANES * (m + 1)]
        gl_ref[:, LANES * m:LANES * (m + 1)] = _gelu_tanh(y_m)
    gl = gl_ref[...]
    out_a = gl * _sigmoid(_dot(gl.astype(BF16), wglu_ref[...]) + bglu_ref[...])
    ycat_ref[:, 0:S5_WIDTH] = (out_a * _silu(proj_ref[:, S5_WIDTH:2 * S5_WIDTH])).astype(BF16)

    o = 2 * S5_WIDTH
    hc_ref[CONV_HALO:CONV_HALO + rows, :] = (
        proj_ref[:, o:o + CONV_WIDTH] * _sigmoid(proj_ref[:, o + CONV_WIDTH:o + 2 * CONV_WIDTH]))
    rc = 32

    def conv_chunk(c, _):
        r0 = pl.multiple_of(c * rc, rc)
        accs = [jnp.zeros((NBATCH, CONV_WIDTH), F32) for _ in range(rc // NBATCH)]
        for k in range(CONV_KERNEL):
            w = cw_ref[k]
            for rr in range(rc // NBATCH):
                accs[rr] = accs[rr] + hc_ref[pl.ds(r0 + NBATCH * (k + rr), NBATCH), :] * w
        h = jnp.concatenate(accs, axis=0) + cb_ref[...]
        mu = jnp.mean(h, axis=-1, keepdims=True)
        var = jnp.mean(jnp.square(h - mu), axis=-1, keepdims=True)
        hf = (h - mu) * lax.rsqrt(var + NORM_EPS) * lng_ref[...] + lnb_ref[...]
        z_b = proj_ref[pl.ds(r0, rc), o + 2 * CONV_WIDTH:o + 3 * CONV_WIDTH]
        ycat_ref[pl.ds(r0, rc), S5_WIDTH:S5_WIDTH + CONV_WIDTH] = (_silu(hf) * _silu(z_b)).astype(BF16)
        return 0

    lax.fori_loop(0, rows // rc, conv_chunk, 0)
    hc_ref[0:CONV_HALO, :] = hc_ref[rows:rows + CONV_HALO, :]

    o_ref[...] = _dot(ycat_ref[...], wout_ref[...])
    for t in range(tt):
        o_ref[NBATCH * t:NBATCH * (t + 1), :] += x_ref[:, t * D_MODEL:(t + 1) * D_MODEL]


def _layer0(x2, gain, w_in, a_re, a_im, bc, cc, d_skip, w_glu, b_glu, conv_w, conv_b, ln_g, ln_b, w_out, *, seq, tt):
    rows = NBATCH * tt
    const = lambda shape: pl.BlockSpec(shape, lambda i: (0,) * len(shape))
    return pl.pallas_call(
        functools.partial(_layer0_kernel, tt=tt),
        grid=(seq // tt,),
        in_specs=[
            pl.BlockSpec((NBATCH, tt * D_MODEL), lambda i: (0, i)),
            const((1, D_MODEL)),
            const((D_MODEL, EVEN_IN)),
            const((NBATCH, S5_NSTATE)),
            const((NBATCH, S5_NSTATE)),
            const((S5_SLABS, LANES, 2 * S5_NSTATE // S5_SLABS)),
            const((S5_SLABS, 2 * S5_NSTATE // S5_SLABS, LANES)),
            const((1, S5_WIDTH)),
            const((S5_WIDTH, S5_WIDTH)),
            const((1, S5_WIDTH)),
            const((CONV_KERNEL, NBATCH, CONV_WIDTH)),
            const((1, CONV_WIDTH)),
            const((1, CONV_WIDTH)),
            const((1, CONV_WIDTH)),
            const((S5_WIDTH + CONV_WIDTH, D_MODEL)),
        ],
        out_specs=pl.BlockSpec((rows, D_MODEL), lambda i: (i, 0)),
        out_shape=jax.ShapeDtypeStruct((seq * NBATCH, D_MODEL), F32),
        scratch_shapes=[
            pltpu.VMEM((rows, D_MODEL), BF16),
            pltpu.VMEM((rows, EVEN_IN), F32),
            pltpu.VMEM((rows, 2 * S5_NSTATE), F32),
            pltpu.VMEM((NBATCH, 2 * S5_NSTATE), F32),
            pltpu.VMEM((CONV_HALO + rows, CONV_WIDTH), F32),
            pltpu.VMEM((rows, S5_WIDTH), F32),
            pltpu.VMEM((rows, S5_WIDTH + CONV_WIDTH), BF16),
        ],
        compiler_params=pltpu.CompilerParams(dimension_semantics=("arbitrary",), vmem_limit_bytes=VMEM_LIMIT),
        name="layer0",
    )(x2, gain, w_in, a_re, a_im, bc, cc, d_skip, w_glu, b_glu, conv_w, conv_b, ln_g, ln_b, w_out)


def _inproj1_kernel(h_ref, g_ref, w_ref, cos_ref, sin_ref,
                    q_ref, k_ref, qi_ref, ki_ref, wi_ref, gate_ref, v_ref):
    h = h_ref[...]
    ms = jnp.mean(h * h, axis=-1, keepdims=True)
    hn = (h * lax.rsqrt(ms + NORM_EPS) * g_ref[...]).astype(BF16)
    proj = _dot(hn, w_ref[...])
    cos_t = cos_ref[...]
    sin_t = sin_ref[...]
    lane = lax.broadcasted_iota(I32, cos_t.shape, 1)
    first = (lane & (HEAD_DIM - 1)) < (ROT_DIM // 2)
    low = lane < HEAD_DIM

    def slab(c):
        return proj[:, LANES * c:LANES * (c + 1)]

    def rope(xs, ct, st):
        partner = jnp.where(first, pltpu.roll(xs, LANES - ROT_DIM // 2, 1), pltpu.roll(xs, ROT_DIM // 2, 1))
        return xs * ct + partner * st

    q_scale = (HEAD_DIM ** -0.5) * LOG2E
    for c in range(8):
        q_ref[:, LANES * c:LANES * (c + 1)] = (rope(slab(c), cos_t, sin_t) * q_scale).astype(BF16)
    for c in range(2):
        k_ref[:, LANES * c:LANES * (c + 1)] = rope(slab(8 + c), cos_t, sin_t).astype(BF16)
    for c in range(4):
        qi_ref[:, LANES * c:LANES * (c + 1)] = rope(slab(10 + c), cos_t, sin_t).astype(BF16)
    kiw = rope(slab(14), jnp.where(low, cos_t, 1.0), jnp.where(low, sin_t, 0.0))
    ki_ref[...] = kiw.astype(BF16)
    wi_ref[...] = kiw
    for c in range(8):
        gate_ref[:, LANES * c:LANES * (c + 1)] = _silu(slab(15 + c)).astype(BF16)
    for c in range(2):
        v_ref[:, LANES * c:LANES * (c + 1)] = slab(23 + c).astype(BF16)


def _inproj1(h1, gain, w, cos_t, sin_t, *, seq, rows):
    n = seq * NBATCH
    const = lambda shape: pl.BlockSpec(shape, lambda i: (0,) * len(shape))
    rb = lambda w_: pl.BlockSpec((rows, w_), lambda i: (i, 0))
    return pl.pallas_call(
        _inproj1_kernel,
        grid=(n // rows,),
        in_specs=[rb(D_MODEL), const((1, D_MODEL)), const((D_MODEL, ODD_PAD)), rb(LANES), rb(LANES)],
        out_specs=[rb(ATT_WIDTH), rb(KV_WIDTH), rb(IDX_HEADS * IDX_DIM), rb(LANES), rb(LANES), rb(ATT_WIDTH),
                   rb(KV_WIDTH)],
        out_shape=[
            jax.ShapeDtypeStruct((n, ATT_WIDTH), BF16),
            jax.ShapeDtypeStruct((n, KV_WIDTH), BF16),
            jax.ShapeDtypeStruct((n, IDX_HEADS * IDX_DIM), BF16),
            jax.ShapeDtypeStruct((n, LANES), BF16),
            jax.ShapeDtypeStruct((n, LANES), F32),
            jax.ShapeDtypeStruct((n, ATT_WIDTH), BF16),
            jax.ShapeDtypeStruct((n, KV_WIDTH), BF16),
        ],
        compiler_params=pltpu.CompilerParams(dimension_semantics=("arbitrary",), vmem_limit_bytes=VMEM_LIMIT),
        name="inproj1",
    )(h1, gain, w, cos_t, sin_t)


def _attn_kernel(q_ref, qi_ref, wi_ref, gate_ref, h1_ref, k_ref, v_ref, ki_ref, wout_ref, fg_ref, o_ref,
                 vt_ref, sc_ref, qh_ref, qih_ref, acc_ref, m_ref, l_ref, jcut_ref, *, seq, ksel):
    tq = ATT_TILE
    i = pl.program_id(1)
    nblk = i + 1

    @pl.when(i == 0)
    def _():
        def transpose_v(kb, _):
            r0 = pl.multiple_of(kb * tq, tq)
            vt_ref[kb] = v_ref[pl.ds(r0, tq), :].astype(F32).T.astype(BF16)
            return 0
        lax.fori_loop(0, seq // tq, transpose_v, 0)

    q = q_ref[...]
    for h in range(N_HEADS):
        j, g = divmod(h, Q_PER_KV)
        qh_ref[j, tq * g:tq * (g + 1), :] = q[:, HEAD_DIM * h:HEAD_DIM * (h + 1)]
    qi = qi_ref[...]
    for h in range(IDX_HEADS):
        qih_ref[tq * h:tq * (h + 1), :] = qi[:, IDX_DIM * h:IDX_DIM * (h + 1)]
    w8 = wi_ref[...].T[IDX_DIM:IDX_DIM + IDX_HEADS, :] * (IDX_HEADS ** -0.5) * (IDX_DIM ** -0.5)

    row = lax.broadcasted_iota(I32, (tq, tq), 0)
    col = lax.broadcasted_iota(I32, (tq, tq), 1)
    row_chunk = row >> 6
    col_chunk = col >> 6

    def score_block(kb, _):
        r0 = pl.multiple_of(kb * tq, tq)
        kib = ki_ref[pl.ds(r0, tq), :][:, 0:IDX_DIM]
        s = _dot_nt(kib, qih_ref[...])
        acc = jnp.maximum(s[:, 0:tq], 0.0) * w8[0:1, :]
        for h in range(1, IDX_HEADS):
            acc = acc + jnp.maximum(s[:, tq * h:tq * (h + 1)], 0.0) * w8[h:h + 1, :]
        later = ((kb - i) * (tq // CHUNK) + row_chunk) > col_chunk
        sc_ref[pl.ds(r0, tq), :] = jnp.where(later, -jnp.inf, acc)
        return 0

    lax.fori_loop(0, nblk, score_block, 0)

    def count(pred):
        def body(kb, acc):
            r0 = pl.multiple_of(kb * tq, tq)
            ones = jnp.where(pred(sc_ref[pl.ds(r0, tq), :], r0), 1, 0).astype(I32)
            part = ones[0:8]
            for r in range(1, tq // 8):
                part = part + ones[8 * r:8 * (r + 1)]
            return acc + part
        acc = lax.fori_loop(0, nblk, body, jnp.zeros((8, tq), I32))
        return jnp.sum(acc, axis=0, keepdims=True)

    def as_threshold(key):
        ks = key ^ INT_MIN
        return pltpu.bitcast(jnp.where(ks >= 0, ks, ks ^ np.int32(0x7FFFFFFF)), F32)

    def bisect(it, t):
        cand = t | (jnp.int32(1) << (31 - it))
        cand_f = as_threshold(cand)
        c = count(lambda blk, r0: blk >= cand_f)
        return jnp.where(c >= ksel, cand, t)

    thr = as_threshold(lax.fori_loop(0, 32, bisect, jnp.zeros((1, tq), I32)))
    thr = jnp.maximum(jnp.where(thr != thr, -F32_MAX, thr), -F32_MAX)

    cnt_gt = count(lambda blk, r0: blk > thr)
    cnt_ge = count(lambda blk, r0: blk >= thr)
    need = ksel - cnt_gt
    jbits = int(seq).bit_length()
    jcut_ref[...] = jnp.full((1, tq), (1 << jbits) - 1, I32)

    @pl.when(jnp.max(jnp.where(cnt_ge > ksel, 1, 0)) > 0)
    def _():
        def jbisect(it, jc):
            cand = jc | (jnp.int32(1) << (jbits - 1 - it))
            f = count(lambda blk, r0: (blk == thr) & ((r0 + row) < cand))
            return jnp.where(f <= need, cand, jc)
        jcut_ref[...] = lax.fori_loop(0, jbits, jbisect, jnp.zeros((1, tq), I32))

    jcut = jcut_ref[...]

    m_ref[...] = jnp.full(m_ref.shape, NEG_BIG, F32)
    l_ref[...] = jnp.zeros(l_ref.shape, F32)
    acc_ref[...] = jnp.zeros(acc_ref.shape, F32)

    def attend(kb, _):
        r0 = pl.multiple_of(kb * tq, tq)
        blk = sc_ref[pl.ds(r0, tq), :]
        sel = (blk > thr) | ((blk == thr) & ((r0 + row) < jcut))
        bias = jnp.where(sel, 0.0, -jnp.inf)
        bias = jnp.concatenate([bias] * Q_PER_KV, axis=1)
        kblk = k_ref[pl.ds(r0, tq), :]
        for j in range(N_KV_HEADS):
            s = _dot_nt(kblk[:, HEAD_DIM * j:HEAD_DIM * (j + 1)], qh_ref[j]) + bias
            m_old = m_ref[j:j + 1, :]
            m_new = jnp.maximum(m_old, jnp.max(s, axis=0, keepdims=True))
            alpha = jnp.exp2(m_old - m_new)
            p = jnp.exp2(s - m_new)
            l_ref[j:j + 1, :] = l_ref[j:j + 1, :] * alpha + jnp.sum(p, axis=0, keepdims=True)
            m_ref[j:j + 1, :] = m_new
            acc_ref[j] = acc_ref[j] * alpha + _dot(vt_ref[kb, HEAD_DIM * j:HEAD_DIM * (j + 1), :], p.astype(BF16))
        return 0

    lax.fori_loop(0, nblk, attend, 0)

    slabs = []
    for pair in range(N_HEADS // 2):
        j, g0 = pair // 2, 2 * (pair % 2)
        inv_l = 1.0 / l_ref[j:j + 1, tq * g0:tq * (g0 + 2)]
        o2 = acc_ref[j, :, tq * g0:tq * (g0 + 2)] * inv_l
        slabs.append(jnp.concatenate([o2[:, 0:tq], o2[:, tq:2 * tq]], axis=0).T)
    att = jnp.concatenate(slabs, axis=1)
    y = _dot((att * gate_ref[...].astype(F32)).astype(BF16), wout_ref[...])
    hh = h1_ref[...] + y
    ms = jnp.mean(hh * hh, axis=-1, keepdims=True)
    o_ref[0] = hh * lax.rsqrt(ms + NORM_EPS) * fg_ref[...]


def _attention(q, qi, wi, gate, h1, k, v, ki, w_out, final_g, *, seq, ksel):
    tq = ATT_TILE
    tile = lambda w_: pl.BlockSpec((tq, w_), lambda b, i: (i, b))
    whole = lambda w_: pl.BlockSpec((seq, w_), lambda b, i: (0, b))
    const = lambda shape: pl.BlockSpec(shape, lambda b, i: (0,) * len(shape))
    as_cols = lambda a: a.reshape(seq, -1)
    return pl.pallas_call(
        functools.partial(_attn_kernel, seq=seq, ksel=ksel),
        grid=(NBATCH, seq // tq),
        in_specs=[tile(ATT_WIDTH), tile(IDX_HEADS * IDX_DIM), tile(LANES), tile(ATT_WIDTH), tile(D_MODEL),
                  whole(KV_WIDTH), whole(KV_WIDTH), whole(LANES),
                  const((ATT_WIDTH, D_MODEL)), const((1, D_MODEL))],
        out_specs=pl.BlockSpec((1, tq, D_MODEL), lambda b, i: (b, i, 0)),
        out_shape=jax.ShapeDtypeStruct((NBATCH, seq, D_MODEL), F32),
        scratch_shapes=[
            pltpu.VMEM((seq // tq, KV_WIDTH, tq), BF16),
            pltpu.VMEM((seq, tq), F32),
            pltpu.VMEM((N_KV_HEADS, Q_PER_KV * tq, HEAD_DIM), BF16),
            pltpu.VMEM((IDX_HEADS * tq, IDX_DIM), BF16),
            pltpu.VMEM((N_KV_HEADS, HEAD_DIM, Q_PER_KV * tq), F32),
            pltpu.VMEM((N_KV_HEADS, Q_PER_KV * tq), F32),
            pltpu.VMEM((N_KV_HEADS, Q_PER_KV * tq), F32),
            pltpu.VMEM((1, tq), I32),
        ],
        compiler_params=pltpu.CompilerParams(dimension_semantics=("arbitrary", "arbitrary"),
                                             vmem_limit_bytes=VMEM_LIMIT),
        name="dsa_attention",
    )(as_cols(q), as_cols(qi), as_cols(wi), as_cols(gate), as_cols(h1), as_cols(k), as_cols(v), as_cols(ki),
      w_out, final_g)


def _rope_tables(seq):
    pos = jnp.arange(seq, dtype=F32)
    inv = ROPE_THETA ** (-jnp.arange(0, ROT_DIM, 2, dtype=F32) / ROT_DIM)
    ang = pos[:, None] * inv[None, :]
    cos, sin = jnp.cos(ang), jnp.sin(ang)
    ones = jnp.ones((seq, HEAD_DIM - ROT_DIM), F32)
    cos_h = jnp.concatenate([cos, cos, ones], axis=1)
    sin_h = jnp.concatenate([-sin, sin, 0.0 * ones], axis=1)
    rep = lambda a: jnp.repeat(jnp.concatenate([a, a], axis=1), NBATCH, axis=0)
    return rep(cos_h), rep(sin_h)


def _reorder_odd_weights(w):
    o = np.cumsum([0, ATT_WIDTH, ATT_WIDTH, KV_WIDTH, KV_WIDTH, IDX_HEADS * IDX_DIM, IDX_DIM, IDX_HEADS]).tolist()
    q, z, k, v, qi, ki, wi = [w[:, o[j]:o[j + 1]] for j in range(7)]
    pad = jnp.zeros((w.shape[0], LANES - IDX_DIM - IDX_HEADS), w.dtype)
    return jnp.concatenate([q, k, qi, ki, wi, pad, z, v], axis=1)


def kernel(x, norm_g, e_w_in, e_lam_re, e_lam_im, e_log_step, e_b_re, e_b_im, e_c_re, e_c_im, e_d_skip,
           e_w_glu, e_b_glu, e_conv_w, e_conv_b, e_ln_g, e_ln_b, e_w_out, o_w_in, o_w_out, final_g):
    bsz, seq, dm = x.shape
    assert bsz == NBATCH and dm == D_MODEL and seq % ATT_TILE == 0
    assert norm_g.shape[0] == 2 and e_w_in.shape[0] == 1 and o_w_in.shape[0] == 1
    tt = 64
    ksel = min(TOPK_MAX, seq // 4)
    row = lambda a: a.reshape(1, -1).astype(F32)

    ab_re, ab_im, bb_re, bb_im = _s5_prep(e_lam_re[0], e_lam_im[0], e_log_step[0], e_b_re[0], e_b_im[0])
    bc, cc = _s5_block_weights(bb_re, bb_im, e_c_re[0], e_c_im[0])
    a_re = jnp.broadcast_to(ab_re.reshape(1, S5_NSTATE), (NBATCH, S5_NSTATE))
    a_im = jnp.broadcast_to(ab_im.reshape(1, S5_NSTATE), (NBATCH, S5_NSTATE))
    conv_w = jnp.broadcast_to(e_conv_w[0][:, None, :], (CONV_KERNEL, NBATCH, CONV_WIDTH))

    h1 = _layer0(x.reshape(NBATCH, seq * D_MODEL), row(norm_g[0]), e_w_in[0].astype(BF16), a_re, a_im, bc, cc,
                 row(e_d_skip[0]), e_w_glu[0].astype(BF16), row(e_b_glu[0]), conv_w, row(e_conv_b[0]),
                 row(e_ln_g[0]), row(e_ln_b[0]), e_w_out[0].astype(BF16), seq=seq, tt=tt)

    cos_t, sin_t = _rope_tables(seq)
    q, k, qi, ki, wi, gate, v = _inproj1(h1, row(norm_g[1]), _reorder_odd_weights(o_w_in[0]).astype(BF16),
                                         cos_t, sin_t, seq=seq, rows=512)
    return _attention(q, qi, wi, gate, h1, k, v, ki, o_w_out[0].astype(BF16), row(final_g), seq=seq, ksel=ksel)
```

```python
import functools
import math

import numpy as np
import jax
import jax.numpy as jnp
from jax import lax
from jax.experimental import pallas as pl
from jax.experimental.pallas import tpu as pltpu

F32 = jnp.float32
BF16 = jnp.bfloat16
I32 = jnp.int32

NBATCH = 8
D_MODEL = 1024
CHUNK = 64
NORM_EPS = 1e-6
S5_WIDTH = 512
S5_GROUP = 16
S5_GROUPS = 32
S5_STATE = 64
S5_NSTATE = S5_GROUPS * S5_STATE
S5_SLABS = 4
CONV_WIDTH = 512
CONV_KERNEL = 31
CONV_HALO = (CONV_KERNEL - 1) * NBATCH
EVEN_IN = 2 * S5_WIDTH + 3 * CONV_WIDTH
N_HEADS = 16
HEAD_DIM = 64
N_KV_HEADS = 4
Q_PER_KV = 4
ATT_WIDTH = 1024
KV_WIDTH = 256
IDX_HEADS = 8
IDX_DIM = 64
TOPK_MAX = 256
ROPE_THETA = 500000.0
ROT_DIM = 16
LANES = 128
ODD_PAD = 3200
ATT_TILE = 256
PV_ROWS = HEAD_DIM + 16
INT_MIN = np.int32(-2 ** 31)
F32_MAX = float(np.finfo(np.float32).max)
NEG_BIG = -1e38
LOG2E = math.log2(math.e)

VMEM_LIMIT = 56 * 1024 * 1024


def _dot(a, b):
    return jnp.dot(a, b, preferred_element_type=F32)


def _dot_nt(a, b):
    return lax.dot_general(a, b, (((1,), (1,)), ((), ())), preferred_element_type=F32)


def _sigmoid(x):
    return 1.0 / (1.0 + jnp.exp(-x))


def _silu(x):
    return x * _sigmoid(x)


def _gelu_tanh(x):
    c = math.sqrt(2.0 / math.pi)
    return 0.5 * x * (1.0 + jnp.tanh(c * (x + 0.044715 * (x * x * x))))


def _s5_prep_kernel(lr_ref, li_ref, ls_ref, br_ref, bi_ref, abr_ref, abi_ref, bbr_ref, bbi_ref):
    lr = jnp.minimum(lr_ref[...], -1e-4)
    li = li_ref[...]
    dt = jnp.exp(ls_ref[...])
    mag = jnp.exp(lr * dt)
    ab_re = mag * jnp.cos(li * dt)
    ab_im = mag * jnp.sin(li * dt)
    den = lr * lr + li * li
    nr = ab_re - 1.0
    ni = ab_im
    k_re = (nr * lr + ni * li) / den
    k_im = (ni * lr - nr * li) / den
    br = br_ref[...]
    bi = bi_ref[...]
    abr_ref[...] = ab_re
    abi_ref[...] = ab_im
    bbr_ref[...] = k_re * br - k_im * bi
    bbi_ref[...] = k_re * bi + k_im * br


def _s5_prep(lam_re, lam_im, log_step, b_re, b_im):
    g, n, c = b_re.shape
    shape2 = (g * n * c // LANES, LANES)
    expand = lambda a: jnp.broadcast_to(a[:, :, None], (g, n, c)).reshape(shape2)
    ls = jnp.broadcast_to(log_step[:, None], (g, n))
    outs = pl.pallas_call(
        _s5_prep_kernel,
        out_shape=[jax.ShapeDtypeStruct(shape2, F32)] * 4,
        name="s5_prep",
    )(expand(lam_re), expand(lam_im), expand(ls), b_re.reshape(shape2), b_im.reshape(shape2))
    ab_re, ab_im, bb_re, bb_im = [o.reshape(g, n, c) for o in outs]
    return ab_re[:, :, 0], ab_im[:, :, 0], bb_re, bb_im


def _s5_block_weights(bb_re, bb_im, c_re, c_im):
    gl = S5_GROUPS // S5_SLABS
    eye = jnp.eye(gl, dtype=F32)

    def in_map(bb):
        t = bb.reshape(S5_SLABS, gl, S5_STATE, S5_GROUP).transpose(0, 1, 3, 2)
        t = t[:, :, :, None, :] * eye[None, :, None, :, None]
        return t.reshape(S5_SLABS, gl * S5_GROUP, gl * S5_STATE)

    def out_map(cc):
        t = cc.reshape(S5_SLABS, gl, S5_GROUP, S5_STATE).transpose(0, 1, 3, 2)
        t = t[:, :, :, None, :] * eye[None, :, None, :, None]
        return t.reshape(S5_SLABS, gl * S5_STATE, gl * S5_GROUP)

    bc = jnp.concatenate([in_map(bb_re), in_map(bb_im)], axis=2).astype(BF16)
    cc = jnp.concatenate([out_map(c_re), out_map(-c_im)], axis=1).astype(BF16)
    return bc, cc


def _layer0_kernel(x_ref, g_ref, win_ref, are_ref, aim_ref, bc_ref, cc_ref, dsk_ref, wglu_ref, bglu_ref,
                   cw_ref, cb_ref, lng_ref, lnb_ref, wout_ref, o_ref,
                   hn_ref, proj_ref, xs_ref, st_ref, hc_ref, gl_ref, ycat_ref, *, tt):
    rows = NBATCH * tt
    half = S5_NSTATE
    slab_w = half // S5_SLABS

    @pl.when(pl.program_id(0) == 0)
    def _():
        st_ref[...] = jnp.zeros_like(st_ref)
        hc_ref[0:CONV_HALO, :] = jnp.zeros((CONV_HALO, CONV_WIDTH), F32)

    gain = g_ref[...]
    for tp in range(tt // 2):
        xa = x_ref[:, (2 * tp) * D_MODEL:(2 * tp + 1) * D_MODEL]
        xb = x_ref[:, (2 * tp + 1) * D_MODEL:(2 * tp + 2) * D_MODEL]
        xx = jnp.concatenate([xa, xb], axis=0)
        ms = jnp.mean(xx * xx, axis=-1, keepdims=True)
        hn_ref[16 * tp:16 * tp + 16, :] = (xx * lax.rsqrt(ms + NORM_EPS) * gain).astype(BF16)

    proj_ref[...] = _dot(hn_ref[...], win_ref[...])

    for m in range(S5_SLABS):
        u_m = proj_ref[:, LANES * m:LANES * (m + 1)].astype(BF16)
        xm = _dot(u_m, bc_ref[m])
        xs_ref[:, slab_w * m:slab_w * (m + 1)] = xm[:, :slab_w]
        xs_ref[:, half + slab_w * m:half + slab_w * (m + 1)] = xm[:, slab_w:]

    cw = 512
    for cc_i in range(half // cw):
        lo = cc_i * cw
        a_re = are_ref[:, lo:lo + cw]
        a_im = aim_ref[:, lo:lo + cw]

        def step(t, carry, lo=lo, a_re=a_re, a_im=a_im):
            s_re, s_im = carry
            r0 = pl.multiple_of(t * NBATCH, NBATCH)
            x_re = xs_ref[pl.ds(r0, NBATCH), lo:lo + cw]
            x_im = xs_ref[pl.ds(r0, NBATCH), half + lo:half + lo + cw]
            n_re = a_re * s_re - a_im * s_im + x_re
            n_im = a_re * s_im + a_im * s_re + x_im
            xs_ref[pl.ds(r0, NBATCH), lo:lo + cw] = n_re
            xs_ref[pl.ds(r0, NBATCH), half + lo:half + lo + cw] = n_im
            return n_re, n_im

        s_re, s_im = lax.fori_loop(0, tt, step, (st_ref[:, lo:lo + cw], st_ref[:, half + lo:half + lo + cw]),
                                   unroll=8)
        st_ref[:, lo:lo + cw] = s_re
        st_ref[:, half + lo:half + lo + cw] = s_im

    for m in range(S5_SLABS):
        s_re = xs_ref[:, slab_w * m:slab_w * (m + 1)].astype(BF16)
        s_im = xs_ref[:, half + slab_w * m:half + slab_w * (m + 1)].astype(BF16)
        y_m = _dot(s_re, cc_ref[m, 0:slab_w, :]) + _dot(s_im, cc_ref[m, slab_w:2 * slab_w, :])
        y_m = y_m + proj_ref[:, LANES * m:LANES * (m + 1)] * dsk_ref[:, LANES * m:LANES * (m + 1)]
        gl_ref[:, LANES * m:LANES * (m + 1)] = _gelu_tanh(y_m)
    gl = gl_ref[...]
    out_a = gl * _sigmoid(_dot(gl.astype(BF16), wglu_ref[...]) + bglu_ref[...])
    ycat_ref[:, 0:S5_WIDTH] = (out_a * _silu(proj_ref[:, S5_WIDTH:2 * S5_WIDTH])).astype(BF16)

    o = 2 * S5_WIDTH
    hc_ref[CONV_HALO:CONV_HALO + rows, :] = (
        proj_ref[:, o:o + CONV_WIDTH] * _sigmoid(proj_ref[:, o + CONV_WIDTH:o + 2 * CONV_WIDTH]))
    rc = 64
    for cl in range(CONV_WIDTH // LANES):
        lanes = slice(LANES * cl, LANES * (cl + 1))
        taps = [cw_ref[k, :, lanes] for k in range(CONV_KERNEL)]

        def conv_chunk(c, _, lanes=lanes, taps=taps):
            r0 = pl.multiple_of(c * rc, rc)
            accs = [None] * (rc // NBATCH)
            for t in range(rc // NBATCH + CONV_KERNEL - 1):
                h = hc_ref[pl.ds(r0 + NBATCH * t, NBATCH), lanes]
                for rr in range(rc // NBATCH):
                    k = t - rr
                    if 0 <= k < CONV_KERNEL:
                        accs[rr] = h * taps[k] if accs[rr] is None else accs[rr] + h * taps[k]
            gl_ref[pl.ds(r0, rc), lanes] = jnp.concatenate(accs, axis=0)
            return 0

        lax.fori_loop(0, rows // rc, conv_chunk, 0)
    hc_ref[0:CONV_HALO, :] = hc_ref[rows:rows + CONV_HALO, :]
    h = gl_ref[...] + cb_ref[...]
    mu = jnp.mean(h, axis=-1, keepdims=True)
    var = jnp.mean(jnp.square(h - mu), axis=-1, keepdims=True)
    hf = (h - mu) * lax.rsqrt(var + NORM_EPS) * lng_ref[...] + lnb_ref[...]
    z_b = proj_ref[:, o + 2 * CONV_WIDTH:o + 3 * CONV_WIDTH]
    ycat_ref[:, S5_WIDTH:S5_WIDTH + CONV_WIDTH] = (_silu(hf) * _silu(z_b)).astype(BF16)

    y = _dot(ycat_ref[...], wout_ref[...])
    for t in range(tt):
        o_ref[:, t * D_MODEL:(t + 1) * D_MODEL] = (
            y[NBATCH * t:NBATCH * (t + 1), :] + x_ref[:, t * D_MODEL:(t + 1) * D_MODEL])


def _layer0(x2, gain, w_in, a_re, a_im, bc, cc, d_skip, w_glu, b_glu, conv_w, conv_b, ln_g, ln_b, w_out, *, seq, tt):
    rows = NBATCH * tt
    const = lambda shape: pl.BlockSpec(shape, lambda i: (0,) * len(shape))
    return pl.pallas_call(
        functools.partial(_layer0_kernel, tt=tt),
        grid=(seq // tt,),
        in_specs=[
            pl.BlockSpec((NBATCH, tt * D_MODEL), lambda i: (0, i)),
            const((1, D_MODEL)),
            const((D_MODEL, EVEN_IN)),
            const((NBATCH, S5_NSTATE)),
            const((NBATCH, S5_NSTATE)),
            const((S5_SLABS, LANES, 2 * S5_NSTATE // S5_SLABS)),
            const((S5_SLABS, 2 * S5_NSTATE // S5_SLABS, LANES)),
            const((1, S5_WIDTH)),
            const((S5_WIDTH, S5_WIDTH)),
            const((1, S5_WIDTH)),
            const((CONV_KERNEL, NBATCH, CONV_WIDTH)),
            const((1, CONV_WIDTH)),
            const((1, CONV_WIDTH)),
            const((1, CONV_WIDTH)),
            const((S5_WIDTH + CONV_WIDTH, D_MODEL)),
        ],
        out_specs=pl.BlockSpec((NBATCH, tt * D_MODEL), lambda i: (0, i)),
        out_shape=jax.ShapeDtypeStruct((NBATCH, seq * D_MODEL), F32),
        scratch_shapes=[
            pltpu.VMEM((rows, D_MODEL), BF16),
            pltpu.VMEM((rows, EVEN_IN), F32),
            pltpu.VMEM((rows, 2 * S5_NSTATE), F32),
            pltpu.VMEM((NBATCH, 2 * S5_NSTATE), F32),
            pltpu.VMEM((CONV_HALO + rows, CONV_WIDTH), F32),
            pltpu.VMEM((rows, S5_WIDTH), F32),
            pltpu.VMEM((rows, S5_WIDTH + CONV_WIDTH), BF16),
        ],
        compiler_params=pltpu.CompilerParams(dimension_semantics=("arbitrary",), vmem_limit_bytes=VMEM_LIMIT),
        name="layer0",
    )(x2, gain, w_in, a_re, a_im, bc, cc, d_skip, w_glu, b_glu, conv_w, conv_b, ln_g, ln_b, w_out)


def _inproj1_kernel(h_ref, g_ref, w_ref, cos_ref, sin_ref,
                    q_ref, k_ref, qi_ref, ki_ref, wi_ref, gate_ref, v_ref):
    h = h_ref[...]
    ms = jnp.mean(h * h, axis=-1, keepdims=True)
    hn = (h * lax.rsqrt(ms + NORM_EPS) * g_ref[...]).astype(BF16)
    proj = _dot(hn, w_ref[...])
    cos_t = cos_ref[...]
    sin_t = sin_ref[...]
    lane = lax.broadcasted_iota(I32, cos_t.shape, 1)
    first = (lane & (HEAD_DIM - 1)) < (ROT_DIM // 2)
    low = lane < HEAD_DIM

    def slab(c):
        return proj[:, LANES * c:LANES * (c + 1)]

    def rope(xs, ct, st):
        partner = jnp.where(first, pltpu.roll(xs, LANES - ROT_DIM // 2, 1), pltpu.roll(xs, ROT_DIM // 2, 1))
        return xs * ct + partner * st

    q_scale = (HEAD_DIM ** -0.5) * LOG2E
    for c in range(8):
        q_ref[:, LANES * c:LANES * (c + 1)] = (rope(slab(c), cos_t, sin_t) * q_scale).astype(BF16)
    for c in range(2):
        k_ref[:, LANES * c:LANES * (c + 1)] = rope(slab(8 + c), cos_t, sin_t).astype(BF16)
    for c in range(4):
        qi_ref[:, LANES * c:LANES * (c + 1)] = rope(slab(10 + c), cos_t, sin_t).astype(BF16)
    kiw = rope(slab(14), jnp.where(low, cos_t, 1.0), jnp.where(low, sin_t, 0.0))
    ki_ref[...] = kiw.astype(BF16)
    wi_ref[...] = kiw
    for c in range(8):
        gate_ref[:, LANES * c:LANES * (c + 1)] = _silu(slab(15 + c)).astype(BF16)
    for c in range(2):
        v_ref[:, LANES * c:LANES * (c + 1)] = slab(23 + c).astype(BF16)


def _inproj1(h1, gain, w, cos_t, sin_t, *, seq, rows):
    n = seq * NBATCH
    const = lambda shape: pl.BlockSpec(shape, lambda i: (0,) * len(shape))
    rb = lambda w_: pl.BlockSpec((rows, w_), lambda i: (i, 0))
    pos = pl.BlockSpec((rows, LANES), lambda i: (i % (seq // rows), 0))
    return pl.pallas_call(
        _inproj1_kernel,
        grid=(n // rows,),
        in_specs=[rb(D_MODEL), const((1, D_MODEL)), const((D_MODEL, ODD_PAD)), pos, pos],
        out_specs=[rb(ATT_WIDTH), rb(KV_WIDTH), rb(IDX_HEADS * IDX_DIM), rb(LANES), rb(LANES), rb(ATT_WIDTH),
                   rb(KV_WIDTH)],
        out_shape=[
            jax.ShapeDtypeStruct((n, ATT_WIDTH), BF16),
            jax.ShapeDtypeStruct((n, KV_WIDTH), BF16),
            jax.ShapeDtypeStruct((n, IDX_HEADS * IDX_DIM), BF16),
            jax.ShapeDtypeStruct((n, LANES), BF16),
            jax.ShapeDtypeStruct((n, LANES), F32),
            jax.ShapeDtypeStruct((n, ATT_WIDTH), BF16),
            jax.ShapeDtypeStruct((n, KV_WIDTH), BF16),
        ],
        compiler_params=pltpu.CompilerParams(dimension_semantics=("arbitrary",), vmem_limit_bytes=VMEM_LIMIT),
        name="inproj1",
    )(h1, gain, w, cos_t, sin_t)


def _attn_kernel(q_ref, qi_ref, wi_ref, gate_ref, h1_ref, k_ref, v_ref, ki_ref, wout_ref, fg_ref, o_ref,
                 vt_ref, sc_ref, qh_ref, qih_ref, acc_ref, m_ref, jcut_ref, s_ref, mb_ref, *, seq, ksel):
    tq = ATT_TILE
    i = pl.program_id(1)
    nblk = i + 1

    @pl.when(i == 0)
    def _():
        def transpose_v(kb, _):
            r0 = pl.multiple_of(kb * tq, tq)
            vt = v_ref[pl.ds(r0, tq), :].astype(F32).T
            for j in range(N_KV_HEADS):
                vt_ref[kb, j, 0:HEAD_DIM, :] = vt[HEAD_DIM * j:HEAD_DIM * (j + 1), :].astype(BF16)
                vt_ref[kb, j, HEAD_DIM:PV_ROWS, :] = jnp.ones((PV_ROWS - HEAD_DIM, tq), BF16)
            return 0
        lax.fori_loop(0, seq // tq, transpose_v, 0)

    zero_half = jnp.zeros((HEAD_DIM, tq), BF16)
    for c in range(ATT_WIDTH // LANES):
        t = q_ref[:, LANES * c:LANES * (c + 1)].astype(F32).T
        for half in range(2):
            j, g = divmod(2 * c + half, Q_PER_KV)
            lo = (j % 2) * HEAD_DIM
            cols = slice(tq * g, tq * (g + 1))
            qh_ref[j, lo:lo + HEAD_DIM, cols] = t[HEAD_DIM * half:HEAD_DIM * (half + 1), :].astype(BF16)
            qh_ref[j, HEAD_DIM - lo:2 * HEAD_DIM - lo, cols] = zero_half
    for c in range(IDX_HEADS * IDX_DIM // LANES):
        t = qi_ref[:, LANES * c:LANES * (c + 1)].astype(F32).T
        for half in range(2):
            h = 2 * c + half
            qih_ref[0:IDX_DIM, tq * h:tq * (h + 1)] = t[IDX_DIM * half:IDX_DIM * (half + 1), :].astype(BF16)
    qih_ref[IDX_DIM:LANES, :] = jnp.zeros((LANES - IDX_DIM, IDX_HEADS * tq), BF16)
    w8 = wi_ref[...].T[IDX_DIM:IDX_DIM + IDX_HEADS, :] * (IDX_HEADS ** -0.5) * (IDX_DIM ** -0.5)

    row = lax.broadcasted_iota(I32, (tq, tq), 0)
    col = lax.broadcasted_iota(I32, (tq, tq), 1)
    row_chunk = row >> 6
    col_chunk = col >> 6

    def score_block(kb, _):
        r0 = pl.multiple_of(kb * tq, tq)
        s = _dot(ki_ref[pl.ds(r0, tq), :], qih_ref[...])
        acc = jnp.maximum(s[:, 0:tq], 0.0) * w8[0:1, :]
        for h in range(1, IDX_HEADS):
            acc = acc + jnp.maximum(s[:, tq * h:tq * (h + 1)], 0.0) * w8[h:h + 1, :]
        later = ((kb - i) * (tq // CHUNK) + row_chunk) > col_chunk
        sc_ref[pl.ds(r0, tq), :] = jnp.where(later, -jnp.inf, acc)
        return 0

    lax.fori_loop(0, nblk, score_block, 0)

    def count(pred):
        def body(kb, acc):
            r0 = pl.multiple_of(kb * tq, tq)
            ones = jnp.where(pred(sc_ref[pl.ds(r0, tq), :], r0), 1, 0).astype(I32)
            parts = [ones[8 * r:8 * (r + 1)] for r in range(tq // 8)]
            while len(parts) > 1:
                parts = [parts[a] + parts[a + 1] for a in range(0, len(parts), 2)]
            return acc + parts[0]
        acc = lax.fori_loop(0, nblk, body, jnp.zeros((8, tq), I32))
        return jnp.sum(acc, axis=0, keepdims=True)

    def as_threshold(key):
        ks = key ^ INT_MIN
        return pltpu.bitcast(jnp.where(ks >= 0, ks, ks ^ np.int32(0x7FFFFFFF)), F32)

    def bisect(it, t):
        cand = t | (jnp.int32(1) << (31 - it))
        cand_f = as_threshold(cand)
        c = count(lambda blk, r0: blk >= cand_f)
        return jnp.where(c >= ksel, cand, t)

    thr = as_threshold(lax.fori_loop(0, 32, bisect, jnp.zeros((1, tq), I32)))
    thr = jnp.maximum(jnp.where(thr != thr, -F32_MAX, thr), -F32_MAX)

    cnt_gt = count(lambda blk, r0: blk > thr)
    cnt_ge = count(lambda blk, r0: blk >= thr)
    need = ksel - cnt_gt
    jbits = int(seq).bit_length()
    jcut_ref[...] = jnp.full((1, tq), (1 << jbits) - 1, I32)

    @pl.when(jnp.max(jnp.where(cnt_ge > ksel, 1, 0)) > 0)
    def _():
        def jbisect(it, jc):
            cand = jc | (jnp.int32(1) << (jbits - 1 - it))
            f = count(lambda blk, r0: (blk == thr) & ((r0 + row) < cand))
            return jnp.where(f <= need, cand, jc)
        jcut_ref[...] = lax.fori_loop(0, jbits, jbisect, jnp.zeros((1, tq), I32))

    jcut = jcut_ref[...]

    m_ref[...] = jnp.full(m_ref.shape, NEG_BIG, F32)
    acc_ref[...] = jnp.zeros(acc_ref.shape, F32)

    def logits(kb, slot):
        r0 = pl.multiple_of(kb * tq, tq)
        blk = sc_ref[pl.ds(r0, tq), :]
        sel = (blk > thr) | ((blk == thr) & ((r0 + row) < jcut))
        bias = jnp.where(sel, 0.0, -jnp.inf)
        bias = jnp.concatenate([bias] * Q_PER_KV, axis=1)
        for j in range(N_KV_HEADS):
            s = _dot(k_ref[pl.ds(r0, tq), LANES * (j // 2):LANES * (j // 2 + 1)], qh_ref[j]) + bias
            s_ref[slot, j] = s
            mb_ref[slot, j:j + 1, :] = jnp.max(s, axis=0, keepdims=True)

    def accumulate(kb, slot):
        for j in range(N_KV_HEADS):
            m_old = m_ref[j:j + 1, :]
            m_new = jnp.maximum(m_old, mb_ref[slot, j:j + 1, :])
            m_ref[j:j + 1, :] = m_new
            p = jnp.exp2(s_ref[slot, j] - m_new).astype(BF16)
            acc_ref[j] = acc_ref[j] * jnp.exp2(m_old - m_new) + _dot(vt_ref[kb, j], p)

    def attend_pair(t, _):
        kb = 2 * t
        logits(kb + 1, 1)
        accumulate(kb, 0)
        logits(kb + 2, 0)
        accumulate(kb + 1, 1)
        return 0

    logits(0, 0)
    npair = (nblk - 1) // 2
    lax.fori_loop(0, npair, attend_pair, 0)
    done = 2 * npair

    @pl.when(nblk - done == 1)
    def _():
        accumulate(done, 0)

    @pl.when(nblk - done == 2)
    def _():
        logits(done + 1, 1)
        accumulate(done, 0)
        accumulate(done + 1, 1)

    slabs = []
    for pair in range(N_HEADS // 2):
        j, g0 = pair // 2, 2 * (pair % 2)
        inv_l = 1.0 / acc_ref[j, HEAD_DIM:HEAD_DIM + 1, tq * g0:tq * (g0 + 2)]
        o2 = acc_ref[j, 0:HEAD_DIM, tq * g0:tq * (g0 + 2)] * inv_l
        slabs.append(jnp.concatenate([o2[:, 0:tq], o2[:, tq:2 * tq]], axis=0).T)
    att = jnp.concatenate(slabs, axis=1)
    y = _dot((att * gate_ref[...].astype(F32)).astype(BF16), wout_ref[...])
    hh = h1_ref[...] + y
    ms = jnp.mean(hh * hh, axis=-1, keepdims=True)
    o_ref[0] = hh * lax.rsqrt(ms + NORM_EPS) * fg_ref[...]


def _attention(q, qi, wi, gate, h1, k, v, ki, w_out, final_g, *, seq, ksel):
    tq = ATT_TILE
    ntile = seq // tq
    tile = lambda w_: pl.BlockSpec((tq, w_), lambda b, i: (b * ntile + i, 0))
    whole = lambda w_: pl.BlockSpec((seq, w_), lambda b, i: (b, 0))
    const = lambda shape: pl.BlockSpec(shape, lambda b, i: (0,) * len(shape))
    return pl.pallas_call(
        functools.partial(_attn_kernel, seq=seq, ksel=ksel),
        grid=(NBATCH, seq // tq),
        in_specs=[tile(ATT_WIDTH), tile(IDX_HEADS * IDX_DIM), tile(LANES), tile(ATT_WIDTH), tile(D_MODEL),
                  whole(KV_WIDTH), whole(KV_WIDTH), whole(LANES),
                  const((ATT_WIDTH, D_MODEL)), const((1, D_MODEL))],
        out_specs=pl.BlockSpec((1, tq, D_MODEL), lambda b, i: (b, i, 0)),
        out_shape=jax.ShapeDtypeStruct((NBATCH, seq, D_MODEL), F32),
        scratch_shapes=[
            pltpu.VMEM((seq // tq, N_KV_HEADS, PV_ROWS, tq), BF16),
            pltpu.VMEM((seq, tq), F32),
            pltpu.VMEM((N_KV_HEADS, LANES, Q_PER_KV * tq), BF16),
            pltpu.VMEM((LANES, IDX_HEADS * tq), BF16),
            pltpu.VMEM((N_KV_HEADS, PV_ROWS, Q_PER_KV * tq), F32),
            pltpu.VMEM((N_KV_HEADS, Q_PER_KV * tq), F32),
            pltpu.VMEM((1, tq), I32),
            pltpu.VMEM((2, N_KV_HEADS, tq, Q_PER_KV * tq), F32),
            pltpu.VMEM((2, N_KV_HEADS, Q_PER_KV * tq), F32),
        ],
        compiler_params=pltpu.CompilerParams(dimension_semantics=("arbitrary", "arbitrary"),
                                             vmem_limit_bytes=VMEM_LIMIT),
        name="dsa_attention",
    )(q, qi, wi, gate, h1, k, v, ki, w_out, final_g)


def _rope_tables(seq):
    pos = jnp.arange(seq, dtype=F32)
    inv = ROPE_THETA ** (-jnp.arange(0, ROT_DIM, 2, dtype=F32) / ROT_DIM)
    ang = pos[:, None] * inv[None, :]
    cos, sin = jnp.cos(ang), jnp.sin(ang)
    ones = jnp.ones((seq, HEAD_DIM - ROT_DIM), F32)
    cos_h = jnp.concatenate([cos, cos, ones], axis=1)
    sin_h = jnp.concatenate([-sin, sin, 0.0 * ones], axis=1)
    return jnp.concatenate([cos_h, cos_h], axis=1), jnp.concatenate([sin_h, sin_h], axis=1)


def _reorder_odd_weights(w):
    o = np.cumsum([0, ATT_WIDTH, ATT_WIDTH, KV_WIDTH, KV_WIDTH, IDX_HEADS * IDX_DIM, IDX_DIM, IDX_HEADS]).tolist()
    q, z, k, v, qi, ki, wi = [w[:, o[j]:o[j + 1]] for j in range(7)]
    pad = jnp.zeros((w.shape[0], LANES - IDX_DIM - IDX_HEADS), w.dtype)
    return jnp.concatenate([q, k, qi, ki, wi, pad, z, v], axis=1)


def kernel(x, norm_g, e_w_in, e_lam_re, e_lam_im, e_log_step, e_b_re, e_b_im, e_c_re, e_c_im, e_d_skip,
           e_w_glu, e_b_glu, e_conv_w, e_conv_b, e_ln_g, e_ln_b, e_w_out, o_w_in, o_w_out, final_g):
    bsz, seq, dm = x.shape
    assert bsz == NBATCH and dm == D_MODEL and seq % ATT_TILE == 0
    assert norm_g.shape[0] == 2 and e_w_in.shape[0] == 1 and o_w_in.shape[0] == 1
    tt = 64
    ksel = min(TOPK_MAX, seq // 4)
    row = lambda a: a.reshape(1, -1).astype(F32)

    ab_re, ab_im, bb_re, bb_im = _s5_prep(e_lam_re[0], e_lam_im[0], e_log_step[0], e_b_re[0], e_b_im[0])
    bc, cc = _s5_block_weights(bb_re, bb_im, e_c_re[0], e_c_im[0])
    a_re = jnp.broadcast_to(ab_re.reshape(1, S5_NSTATE), (NBATCH, S5_NSTATE))
    a_im = jnp.broadcast_to(ab_im.reshape(1, S5_NSTATE), (NBATCH, S5_NSTATE))
    conv_w = jnp.broadcast_to(e_conv_w[0][:, None, :], (CONV_KERNEL, NBATCH, CONV_WIDTH))

    h1 = _layer0(x.reshape(NBATCH, seq * D_MODEL), row(norm_g[0]), e_w_in[0].astype(BF16), a_re, a_im, bc, cc,
                 row(e_d_skip[0]), e_w_glu[0].astype(BF16), row(e_b_glu[0]), conv_w, row(e_conv_b[0]),
                 row(e_ln_g[0]), row(e_ln_b[0]), e_w_out[0].astype(BF16), seq=seq, tt=tt)
    h1 = h1.reshape(NBATCH * seq, D_MODEL)

    cos_t, sin_t = _rope_tables(seq)
    q, k, qi, ki, wi, gate, v = _inproj1(h1, row(norm_g[1]), _reorder_odd_weights(o_w_in[0]).astype(BF16),
                                         cos_t, sin_t, seq=seq, rows=512)
    return _attention(q, qi, wi, gate, h1, k, v, ki, o_w_out[0].astype(BF16), row(final_g), seq=seq, ksel=ksel)
```

```python
import functools
import math

import numpy as np
import jax
import jax.numpy as jnp
from jax import lax
from jax.experimental import pallas as pl
from jax.experimental.pallas import tpu as pltpu

F32 = jnp.float32
BF16 = jnp.bfloat16
I32 = jnp.int32

NBATCH = 8
D_MODEL = 1024
CHUNK = 64
NORM_EPS = 1e-6
S5_WIDTH = 512
S5_GROUP = 16
S5_GROUPS = 32
S5_STATE = 64
S5_NSTATE = S5_GROUPS * S5_STATE
S5_SLABS = 4
CONV_WIDTH = 512
CONV_KERNEL = 31
CONV_HALO = (CONV_KERNEL - 1) * NBATCH
EVEN_IN = 2 * S5_WIDTH + 3 * CONV_WIDTH
N_HEADS = 16
HEAD_DIM = 64
N_KV_HEADS = 4
Q_PER_KV = 4
ATT_WIDTH = 1024
KV_WIDTH = 256
IDX_HEADS = 8
IDX_DIM = 64
TOPK_MAX = 256
ROPE_THETA = 500000.0
ROT_DIM = 16
LANES = 128
ODD_PAD = 3200
ATT_TILE = 256
PV_ROWS = HEAD_DIM + 16
INT_MIN = np.int32(-2 ** 31)
F32_MAX = float(np.finfo(np.float32).max)
NEG_BIG = -1e38
LOG2E = math.log2(math.e)

VMEM_LIMIT = 56 * 1024 * 1024


def _dot(a, b):
    return jnp.dot(a, b, preferred_element_type=F32)


def _dot_nt(a, b):
    return lax.dot_general(a, b, (((1,), (1,)), ((), ())), preferred_element_type=F32)


def _sigmoid(x):
    return 1.0 / (1.0 + jnp.exp(-x))


def _silu(x):
    return x * _sigmoid(x)


def _gelu_tanh(x):
    c = math.sqrt(2.0 / math.pi)
    return 0.5 * x * (1.0 + jnp.tanh(c * (x + 0.044715 * (x * x * x))))


def _s5_prep_kernel(lr_ref, li_ref, ls_ref, br_ref, bi_ref, abr_ref, abi_ref, bbr_ref, bbi_ref):
    lr = jnp.minimum(lr_ref[...], -1e-4)
    li = li_ref[...]
    dt = jnp.exp(ls_ref[...])
    mag = jnp.exp(lr * dt)
    ab_re = mag * jnp.cos(li * dt)
    ab_im = mag * jnp.sin(li * dt)
    den = lr * lr + li * li
    nr = ab_re - 1.0
    ni = ab_im
    k_re = (nr * lr + ni * li) / den
    k_im = (ni * lr - nr * li) / den
    br = br_ref[...]
    bi = bi_ref[...]
    abr_ref[...] = ab_re
    abi_ref[...] = ab_im
    bbr_ref[...] = k_re * br - k_im * bi
    bbi_ref[...] = k_re * bi + k_im * br


def _s5_prep(lam_re, lam_im, log_step, b_re, b_im):
    g, n, c = b_re.shape
    shape2 = (g * n * c // LANES, LANES)
    expand = lambda a: jnp.broadcast_to(a[:, :, None], (g, n, c)).reshape(shape2)
    ls = jnp.broadcast_to(log_step[:, None], (g, n))
    outs = pl.pallas_call(
        _s5_prep_kernel,
        out_shape=[jax.ShapeDtypeStruct(shape2, F32)] * 4,
        name="s5_prep",
    )(expand(lam_re), expand(lam_im), expand(ls), b_re.reshape(shape2), b_im.reshape(shape2))
    ab_re, ab_im, bb_re, bb_im = [o.reshape(g, n, c) for o in outs]
    return ab_re[:, :, 0], ab_im[:, :, 0], bb_re, bb_im


def _s5_block_weights(bb_re, bb_im, c_re, c_im):
    gl = S5_GROUPS // S5_SLABS
    eye = jnp.eye(gl, dtype=F32)

    def in_map(bb):
        t = bb.reshape(S5_SLABS, gl, S5_STATE, S5_GROUP).transpose(0, 1, 3, 2)
        t = t[:, :, :, None, :] * eye[None, :, None, :, None]
        return t.reshape(S5_SLABS, gl * S5_GROUP, gl * S5_STATE)

    def out_map(cc):
        t = cc.reshape(S5_SLABS, gl, S5_GROUP, S5_STATE).transpose(0, 1, 3, 2)
        t = t[:, :, :, None, :] * eye[None, :, None, :, None]
        return t.reshape(S5_SLABS, gl * S5_STATE, gl * S5_GROUP)

    bc = jnp.concatenate([in_map(bb_re), in_map(bb_im)], axis=2).astype(BF16)
    cc = jnp.concatenate([out_map(c_re), out_map(-c_im)], axis=1).astype(BF16)
    return bc, cc


def _layer0_kernel(x_ref, g_ref, win_ref, are_ref, aim_ref, bc_ref, cc_ref, dsk_ref, wglu_ref, bglu_ref,
                   cw_ref, cb_ref, lng_ref, lnb_ref, wout_ref, o_ref,
                   hn_ref, proj_ref, xs_ref, st_ref, hc_ref, gl_ref, ycat_ref, tm_ref, *, tt):
    rows = NBATCH * tt
    half = S5_NSTATE
    slab_w = half // S5_SLABS

    @pl.when(pl.program_id(0) == 0)
    def _():
        st_ref[...] = jnp.zeros_like(st_ref)
        hc_ref[0:CONV_HALO, :] = jnp.zeros((CONV_HALO, CONV_WIDTH), F32)

    gain = g_ref[...]
    for tp in range(tt // 2):
        xa = x_ref[:, (2 * tp) * D_MODEL:(2 * tp + 1) * D_MODEL]
        xb = x_ref[:, (2 * tp + 1) * D_MODEL:(2 * tp + 2) * D_MODEL]
        xx = jnp.concatenate([xa, xb], axis=0)
        ms = jnp.mean(xx * xx, axis=-1, keepdims=True)
        hn_ref[16 * tp:16 * tp + 16, :] = (xx * lax.rsqrt(ms + NORM_EPS) * gain).astype(BF16)

    proj_ref[...] = _dot(hn_ref[...], win_ref[...])

    for m in range(S5_SLABS):
        u_m = proj_ref[:, LANES * m:LANES * (m + 1)].astype(BF16)
        xm = _dot(u_m, bc_ref[m])
        xs_ref[:, slab_w * m:slab_w * (m + 1)] = xm[:, :slab_w]
        xs_ref[:, half + slab_w * m:half + slab_w * (m + 1)] = xm[:, slab_w:]

    cw = 512
    for cc_i in range(half // cw):
        lo = cc_i * cw
        a_re = are_ref[:, lo:lo + cw]
        a_im = aim_ref[:, lo:lo + cw]

        def step(t, carry, lo=lo, a_re=a_re, a_im=a_im):
            s_re, s_im = carry
            r0 = pl.multiple_of(t * NBATCH, NBATCH)
            x_re = xs_ref[pl.ds(r0, NBATCH), lo:lo + cw]
            x_im = xs_ref[pl.ds(r0, NBATCH), half + lo:half + lo + cw]
            n_re = a_re * s_re - a_im * s_im + x_re
            n_im = a_re * s_im + a_im * s_re + x_im
            xs_ref[pl.ds(r0, NBATCH), lo:lo + cw] = n_re
            xs_ref[pl.ds(r0, NBATCH), half + lo:half + lo + cw] = n_im
            return n_re, n_im

        s_re, s_im = lax.fori_loop(0, tt, step, (st_ref[:, lo:lo + cw], st_ref[:, half + lo:half + lo + cw]),
                                   unroll=8)
        st_ref[:, lo:lo + cw] = s_re
        st_ref[:, half + lo:half + lo + cw] = s_im

    for m in range(S5_SLABS):
        s_re = xs_ref[:, slab_w * m:slab_w * (m + 1)].astype(BF16)
        s_im = xs_ref[:, half + slab_w * m:half + slab_w * (m + 1)].astype(BF16)
        y_m = _dot(s_re, cc_ref[m, 0:slab_w, :]) + _dot(s_im, cc_ref[m, slab_w:2 * slab_w, :])
        y_m = y_m + proj_ref[:, LANES * m:LANES * (m + 1)] * dsk_ref[:, LANES * m:LANES * (m + 1)]
        gl_ref[:, LANES * m:LANES * (m + 1)] = _gelu_tanh(y_m)
    gl = gl_ref[...]
    out_a = gl * _sigmoid(_dot(gl.astype(BF16), wglu_ref[...]) + bglu_ref[...])
    ycat_ref[:, 0:S5_WIDTH] = (out_a * _silu(proj_ref[:, S5_WIDTH:2 * S5_WIDTH])).astype(BF16)

    o = 2 * S5_WIDTH
    hc_ref[CONV_HALO:CONV_HALO + rows, :] = (
        proj_ref[:, o:o + CONV_WIDTH] * _sigmoid(proj_ref[:, o + CONV_WIDTH:o + 2 * CONV_WIDTH]))
    rc = 64
    for cl in range(CONV_WIDTH // LANES):
        lanes = slice(LANES * cl, LANES * (cl + 1))
        taps = [cw_ref[k, :, lanes] for k in range(CONV_KERNEL)]

        def conv_chunk(c, _, lanes=lanes, taps=taps):
            r0 = pl.multiple_of(c * rc, rc)
            accs = [None] * (rc // NBATCH)
            for t in range(rc // NBATCH + CONV_KERNEL - 1):
                h = hc_ref[pl.ds(r0 + NBATCH * t, NBATCH), lanes]
                for rr in range(rc // NBATCH):
                    k = t - rr
                    if 0 <= k < CONV_KERNEL:
                        accs[rr] = h * taps[k] if accs[rr] is None else accs[rr] + h * taps[k]
            gl_ref[pl.ds(r0, rc), lanes] = jnp.concatenate(accs, axis=0)
            return 0

        lax.fori_loop(0, rows // rc, conv_chunk, 0)
    hc_ref[0:CONV_HALO, :] = hc_ref[rows:rows + CONV_HALO, :]
    h = gl_ref[...] + cb_ref[...]
    mu = jnp.mean(h, axis=-1, keepdims=True)
    var = jnp.mean(jnp.square(h - mu), axis=-1, keepdims=True)
    hf = (h - mu) * lax.rsqrt(var + NORM_EPS) * lng_ref[...] + lnb_ref[...]
    z_b = proj_ref[:, o + 2 * CONV_WIDTH:o + 3 * CONV_WIDTH]
    ycat_ref[:, S5_WIDTH:S5_WIDTH + CONV_WIDTH] = (_silu(hf) * _silu(z_b)).astype(BF16)

    y = _dot(ycat_ref[...], wout_ref[...])
    for t in range(tt):
        h = y[NBATCH * t:NBATCH * (t + 1), :] + x_ref[:, t * D_MODEL:(t + 1) * D_MODEL]
        for c in range(D_MODEL // LANES):
            tm_ref[c, NBATCH * t:NBATCH * (t + 1), :] = h[:, LANES * c:LANES * (c + 1)]
    for b in range(NBATCH):
        for c in range(D_MODEL // LANES):
            o_ref[b, :, LANES * c:LANES * (c + 1)] = tm_ref[c, pl.ds(b, tt, stride=NBATCH), :]


def _layer0(x2, gain, w_in, a_re, a_im, bc, cc, d_skip, w_glu, b_glu, conv_w, conv_b, ln_g, ln_b, w_out, *, seq, tt):
    rows = NBATCH * tt
    const = lambda shape: pl.BlockSpec(shape, lambda i: (0,) * len(shape))
    return pl.pallas_call(
        functools.partial(_layer0_kernel, tt=tt),
        grid=(seq // tt,),
        in_specs=[
            pl.BlockSpec((NBATCH, tt * D_MODEL), lambda i: (0, i)),
            const((1, D_MODEL)),
            const((D_MODEL, EVEN_IN)),
            const((NBATCH, S5_NSTATE)),
            const((NBATCH, S5_NSTATE)),
            const((S5_SLABS, LANES, 2 * S5_NSTATE // S5_SLABS)),
            const((S5_SLABS, 2 * S5_NSTATE // S5_SLABS, LANES)),
            const((1, S5_WIDTH)),
            const((S5_WIDTH, S5_WIDTH)),
            const((1, S5_WIDTH)),
            const((CONV_KERNEL, NBATCH, CONV_WIDTH)),
            const((1, CONV_WIDTH)),
            const((1, CONV_WIDTH)),
            const((1, CONV_WIDTH)),
            const((S5_WIDTH + CONV_WIDTH, D_MODEL)),
        ],
        out_specs=pl.BlockSpec((NBATCH, tt, D_MODEL), lambda i: (0, i, 0)),
        out_shape=jax.ShapeDtypeStruct((NBATCH, seq, D_MODEL), F32),
        scratch_shapes=[
            pltpu.VMEM((rows, D_MODEL), BF16),
            pltpu.VMEM((rows, EVEN_IN), F32),
            pltpu.VMEM((rows, 2 * S5_NSTATE), F32),
            pltpu.VMEM((NBATCH, 2 * S5_NSTATE), F32),
            pltpu.VMEM((CONV_HALO + rows, CONV_WIDTH), F32),
            pltpu.VMEM((rows, S5_WIDTH), F32),
            pltpu.VMEM((rows, S5_WIDTH + CONV_WIDTH), BF16),
            pltpu.VMEM((D_MODEL // LANES, rows, LANES), F32),
        ],
        compiler_params=pltpu.CompilerParams(dimension_semantics=("arbitrary",), vmem_limit_bytes=VMEM_LIMIT),
        name="layer0",
    )(x2, gain, w_in, a_re, a_im, bc, cc, d_skip, w_glu, b_glu, conv_w, conv_b, ln_g, ln_b, w_out)


def _inproj1_kernel(h_ref, g_ref, w_ref, cos_ref, sin_ref,
                    q_ref, k_ref, qi_ref, ki_ref, wi_ref, gate_ref, v_ref):
    h = h_ref[...]
    ms = jnp.mean(h * h, axis=-1, keepdims=True)
    hn = (h * lax.rsqrt(ms + NORM_EPS) * g_ref[...]).astype(BF16)
    proj = _dot(hn, w_ref[...])
    cos_t = cos_ref[...]
    sin_t = sin_ref[...]
    lane = lax.broadcasted_iota(I32, cos_t.shape, 1)
    first = (lane & (HEAD_DIM - 1)) < (ROT_DIM // 2)
    low = lane < HEAD_DIM

    def slab(c):
        return proj[:, LANES * c:LANES * (c + 1)]

    def rope(xs, ct, st):
        partner = jnp.where(first, pltpu.roll(xs, LANES - ROT_DIM // 2, 1), pltpu.roll(xs, ROT_DIM // 2, 1))
        return xs * ct + partner * st

    q_scale = (HEAD_DIM ** -0.5) * LOG2E
    for c in range(8):
        q_ref[:, LANES * c:LANES * (c + 1)] = (rope(slab(c), cos_t, sin_t) * q_scale).astype(BF16)
    for c in range(2):
        k_ref[:, LANES * c:LANES * (c + 1)] = rope(slab(8 + c), cos_t, sin_t).astype(BF16)
    for c in range(4):
        qi_ref[:, LANES * c:LANES * (c + 1)] = rope(slab(10 + c), cos_t, sin_t).astype(BF16)
    kiw = rope(slab(14), jnp.where(low, cos_t, 1.0), jnp.where(low, sin_t, 0.0))
    ki_ref[...] = kiw.astype(BF16)
    wi_ref[...] = kiw
    for c in range(8):
        gate_ref[:, LANES * c:LANES * (c + 1)] = _silu(slab(15 + c)).astype(BF16)
    for c in range(2):
        v_ref[:, LANES * c:LANES * (c + 1)] = slab(23 + c).astype(BF16)


def _inproj1(h1, gain, w, cos_t, sin_t, *, seq, rows):
    n = seq * NBATCH
    const = lambda shape: pl.BlockSpec(shape, lambda i: (0,) * len(shape))
    rb = lambda w_: pl.BlockSpec((rows, w_), lambda i: (i, 0))
    pos = pl.BlockSpec((rows, LANES), lambda i: (i % (seq // rows), 0))
    return pl.pallas_call(
        _inproj1_kernel,
        grid=(n // rows,),
        in_specs=[rb(D_MODEL), const((1, D_MODEL)), const((D_MODEL, ODD_PAD)), pos, pos],
        out_specs=[rb(ATT_WIDTH), rb(KV_WIDTH), rb(IDX_HEADS * IDX_DIM), rb(LANES), rb(LANES), rb(ATT_WIDTH),
                   rb(KV_WIDTH)],
        out_shape=[
            jax.ShapeDtypeStruct((n, ATT_WIDTH), BF16),
            jax.ShapeDtypeStruct((n, KV_WIDTH), BF16),
            jax.ShapeDtypeStruct((n, IDX_HEADS * IDX_DIM), BF16),
            jax.ShapeDtypeStruct((n, LANES), BF16),
            jax.ShapeDtypeStruct((n, LANES), F32),
            jax.ShapeDtypeStruct((n, ATT_WIDTH), BF16),
            jax.ShapeDtypeStruct((n, KV_WIDTH), BF16),
        ],
        compiler_params=pltpu.CompilerParams(dimension_semantics=("arbitrary",), vmem_limit_bytes=VMEM_LIMIT),
        name="inproj1",
    )(h1, gain, w, cos_t, sin_t)


def _attn_kernel(q_ref, qi_ref, wi_ref, gate_ref, h1_ref, k_ref, v_ref, ki_ref, wout_ref, fg_ref, o_ref,
                 vt_ref, sc_ref, qh_ref, qih_ref, acc_ref, m_ref, jcut_ref, s_ref, mb_ref, *, seq, ksel):
    tq = ATT_TILE
    i = pl.program_id(1)
    nblk = i + 1

    @pl.when(i == 0)
    def _():
        def transpose_v(kb, _):
            r0 = pl.multiple_of(kb * tq, tq)
            vt = v_ref[pl.ds(r0, tq), :].astype(F32).T
            for j in range(N_KV_HEADS):
                vt_ref[kb, j, 0:HEAD_DIM, :] = vt[HEAD_DIM * j:HEAD_DIM * (j + 1), :].astype(BF16)
                vt_ref[kb, j, HEAD_DIM:PV_ROWS, :] = jnp.ones((PV_ROWS - HEAD_DIM, tq), BF16)
            return 0
        lax.fori_loop(0, seq // tq, transpose_v, 0)

    zero_half = jnp.zeros((HEAD_DIM, tq), BF16)
    for c in range(ATT_WIDTH // LANES):
        t = q_ref[:, LANES * c:LANES * (c + 1)].astype(F32).T
        for half in range(2):
            j, g = divmod(2 * c + half, Q_PER_KV)
            lo = (j % 2) * HEAD_DIM
            cols = slice(tq * g, tq * (g + 1))
            qh_ref[j, lo:lo + HEAD_DIM, cols] = t[HEAD_DIM * half:HEAD_DIM * (half + 1), :].astype(BF16)
            qh_ref[j, HEAD_DIM - lo:2 * HEAD_DIM - lo, cols] = zero_half
    for c in range(IDX_HEADS * IDX_DIM // LANES):
        t = qi_ref[:, LANES * c:LANES * (c + 1)].astype(F32).T
        for half in range(2):
            h = 2 * c + half
            qih_ref[0:IDX_DIM, tq * h:tq * (h + 1)] = t[IDX_DIM * half:IDX_DIM * (half + 1), :].astype(BF16)
    qih_ref[IDX_DIM:LANES, :] = jnp.zeros((LANES - IDX_DIM, IDX_HEADS * tq), BF16)
    w8 = wi_ref[...].T[IDX_DIM:IDX_DIM + IDX_HEADS, :] * (IDX_HEADS ** -0.5) * (IDX_DIM ** -0.5)

    row = lax.broadcasted_iota(I32, (tq, tq), 0)
    col = lax.broadcasted_iota(I32, (tq, tq), 1)
    row_chunk = row >> 6
    col_chunk = col >> 6

    def score_block(kb, _):
        r0 = pl.multiple_of(kb * tq, tq)
        s = _dot(ki_ref[pl.ds(r0, tq), :], qih_ref[...])
        acc = jnp.maximum(s[:, 0:tq], 0.0) * w8[0:1, :]
        for h in range(1, IDX_HEADS):
            acc = acc + jnp.maximum(s[:, tq * h:tq * (h + 1)], 0.0) * w8[h:h + 1, :]
        later = ((kb - i) * (tq // CHUNK) + row_chunk) > col_chunk
        sc_ref[pl.ds(r0, tq), :] = jnp.where(later, -jnp.inf, acc)
        return 0

    lax.fori_loop(0, nblk, score_block, 0)

    def count(pred):
        def body(kb, acc):
            r0 = pl.multiple_of(kb * tq, tq)
            ones = jnp.where(pred(sc_ref[pl.ds(r0, tq), :], r0), 1, 0).astype(I32)
            parts = [ones[8 * r:8 * (r + 1)] for r in range(tq // 8)]
            while len(parts) > 1:
                parts = [parts[a] + parts[a + 1] for a in range(0, len(parts), 2)]
            return acc + parts[0]
        acc = lax.fori_loop(0, nblk, body, jnp.zeros((8, tq), I32))
        return jnp.sum(acc, axis=0, keepdims=True)

    def as_threshold(key):
        ks = key ^ INT_MIN
        return pltpu.bitcast(jnp.where(ks >= 0, ks, ks ^ np.int32(0x7FFFFFFF)), F32)

    def bisect(it, carry):
        t, cnt_t = carry
        cand = t | (jnp.int32(1) << (31 - it))
        cand_f = as_threshold(cand)
        c = count(lambda blk, r0: blk >= cand_f)
        ok = c >= ksel
        return jnp.where(ok, cand, t), jnp.where(ok, c, cnt_t)

    zero = jnp.zeros((1, tq), I32)
    thr_key, cnt_ge = lax.fori_loop(0, 32, bisect, (zero, zero))
    thr = as_threshold(thr_key)
    thr = jnp.maximum(jnp.where(thr != thr, -F32_MAX, thr), -F32_MAX)

    cnt_gt = count(lambda blk, r0: blk > thr)
    need = ksel - cnt_gt
    jbits = int(seq).bit_length()
    jcut_ref[...] = jnp.full((1, tq), (1 << jbits) - 1, I32)

    @pl.when(jnp.max(jnp.where(cnt_ge > ksel, 1, 0)) > 0)
    def _():
        def jbisect(it, jc):
            cand = jc | (jnp.int32(1) << (jbits - 1 - it))
            f = count(lambda blk, r0: (blk == thr) & ((r0 + row) < cand))
            return jnp.where(f <= need, cand, jc)
        jcut_ref[...] = lax.fori_loop(0, jbits, jbisect, jnp.zeros((1, tq), I32))

    jcut = jcut_ref[...]

    m_ref[...] = jnp.full(m_ref.shape, NEG_BIG, F32)
    acc_ref[...] = jnp.zeros(acc_ref.shape, F32)

    def logits(kb, slot):
        r0 = pl.multiple_of(kb * tq, tq)
        blk = sc_ref[pl.ds(r0, tq), :]
        sel = (blk > thr) | ((blk == thr) & ((r0 + row) < jcut))
        bias = jnp.where(sel, 0.0, -jnp.inf)
        bias = jnp.concatenate([bias] * Q_PER_KV, axis=1)
        for j in range(N_KV_HEADS):
            s = _dot(k_ref[pl.ds(r0, tq), LANES * (j // 2):LANES * (j // 2 + 1)], qh_ref[j]) + bias
            s_ref[slot, j] = s
            mb_ref[slot, j:j + 1, :] = jnp.max(s, axis=0, keepdims=True)

    def accumulate(kb, slot):
        for j in range(N_KV_HEADS):
            m_old = m_ref[j:j + 1, :]
            m_new = jnp.maximum(m_old, mb_ref[slot, j:j + 1, :])
            m_ref[j:j + 1, :] = m_new
            p = jnp.exp2(s_ref[slot, j] - m_new).astype(BF16)
            acc_ref[j] = acc_ref[j] * jnp.exp2(m_old - m_new) + _dot(vt_ref[kb, j], p)

    def attend_pair(t, _):
        kb = 2 * t
        logits(kb + 1, 1)
        accumulate(kb, 0)
        logits(kb + 2, 0)
        accumulate(kb + 1, 1)
        return 0

    logits(0, 0)
    npair = (nblk - 1) // 2
    lax.fori_loop(0, npair, attend_pair, 0)
    done = 2 * npair

    @pl.when(nblk - done == 1)
    def _():
        accumulate(done, 0)

    @pl.when(nblk - done == 2)
    def _():
        logits(done + 1, 1)
        accumulate(done, 0)
        accumulate(done + 1, 1)

    slabs = []
    for pair in range(N_HEADS // 2):
        j, g0 = pair // 2, 2 * (pair % 2)
        inv_l = 1.0 / acc_ref[j, HEAD_DIM:HEAD_DIM + 1, tq * g0:tq * (g0 + 2)]
        o2 = acc_ref[j, 0:HEAD_DIM, tq * g0:tq * (g0 + 2)] * inv_l
        slabs.append(jnp.concatenate([o2[:, 0:tq], o2[:, tq:2 * tq]], axis=0).T)
    att = jnp.concatenate(slabs, axis=1)
    y = _dot((att * gate_ref[...].astype(F32)).astype(BF16), wout_ref[...])
    hh = h1_ref[...] + y
    ms = jnp.mean(hh * hh, axis=-1, keepdims=True)
    o_ref[0] = hh * lax.rsqrt(ms + NORM_EPS) * fg_ref[...]


def _attention(q, qi, wi, gate, h1, k, v, ki, w_out, final_g, *, seq, ksel):
    tq = ATT_TILE
    ntile = seq // tq
    tile = lambda w_: pl.BlockSpec((tq, w_), lambda b, i: (b * ntile + i, 0))
    whole = lambda w_: pl.BlockSpec((seq, w_), lambda b, i: (b, 0))
    const = lambda shape: pl.BlockSpec(shape, lambda b, i: (0,) * len(shape))
    return pl.pallas_call(
        functools.partial(_attn_kernel, seq=seq, ksel=ksel),
        grid=(NBATCH, seq // tq),
        in_specs=[tile(ATT_WIDTH), tile(IDX_HEADS * IDX_DIM), tile(LANES), tile(ATT_WIDTH), tile(D_MODEL),
                  whole(KV_WIDTH), whole(KV_WIDTH), whole(LANES),
                  const((ATT_WIDTH, D_MODEL)), const((1, D_MODEL))],
        out_specs=pl.BlockSpec((1, tq, D_MODEL), lambda b, i: (b, i, 0)),
        out_shape=jax.ShapeDtypeStruct((NBATCH, seq, D_MODEL), F32),
        scratch_shapes=[
            pltpu.VMEM((seq // tq, N_KV_HEADS, PV_ROWS, tq), BF16),
            pltpu.VMEM((seq, tq), F32),
            pltpu.VMEM((N_KV_HEADS, LANES, Q_PER_KV * tq), BF16),
            pltpu.VMEM((LANES, IDX_HEADS * tq), BF16),
            pltpu.VMEM((N_KV_HEADS, PV_ROWS, Q_PER_KV * tq), F32),
            pltpu.VMEM((N_KV_HEADS, Q_PER_KV * tq), F32),
            pltpu.VMEM((1, tq), I32),
            pltpu.VMEM((2, N_KV_HEADS, tq, Q_PER_KV * tq), F32),
            pltpu.VMEM((2, N_KV_HEADS, Q_PER_KV * tq), F32),
        ],
        compiler_params=pltpu.CompilerParams(dimension_semantics=("arbitrary", "arbitrary"),
                                             vmem_limit_bytes=VMEM_LIMIT),
        name="dsa_attention",
    )(q, qi, wi, gate, h1, k, v, ki, w_out, final_g)


def _rope_tables(seq):
    pos = jnp.arange(seq, dtype=F32)
    inv = ROPE_THETA ** (-jnp.arange(0, ROT_DIM, 2, dtype=F32) / ROT_DIM)
    ang = pos[:, None] * inv[None, :]
    cos, sin = jnp.cos(ang), jnp.sin(ang)
    ones = jnp.ones((seq, HEAD_DIM - ROT_DIM), F32)
    cos_h = jnp.concatenate([cos, cos, ones], axis=1)
    sin_h = jnp.concatenate([-sin, sin, 0.0 * ones], axis=1)
    return jnp.concatenate([cos_h, cos_h], axis=1), jnp.concatenate([sin_h, sin_h], axis=1)


def _reorder_odd_weights(w):
    o = np.cumsum([0, ATT_WIDTH, ATT_WIDTH, KV_WIDTH, KV_WIDTH, IDX_HEADS * IDX_DIM, IDX_DIM, IDX_HEADS]).tolist()
    q, z, k, v, qi, ki, wi = [w[:, o[j]:o[j + 1]] for j in range(7)]
    pad = jnp.zeros((w.shape[0], LANES - IDX_DIM - IDX_HEADS), w.dtype)
    return jnp.concatenate([q, k, qi, ki, wi, pad, z, v], axis=1)


def kernel(x, norm_g, e_w_in, e_lam_re, e_lam_im, e_log_step, e_b_re, e_b_im, e_c_re, e_c_im, e_d_skip,
           e_w_glu, e_b_glu, e_conv_w, e_conv_b, e_ln_g, e_ln_b, e_w_out, o_w_in, o_w_out, final_g):
    bsz, seq, dm = x.shape
    assert bsz == NBATCH and dm == D_MODEL and seq % ATT_TILE == 0
    assert norm_g.shape[0] == 2 and e_w_in.shape[0] == 1 and o_w_in.shape[0] == 1
    tt = 64
    ksel = min(TOPK_MAX, seq // 4)
    row = lambda a: a.reshape(1, -1).astype(F32)

    ab_re, ab_im, bb_re, bb_im = _s5_prep(e_lam_re[0], e_lam_im[0], e_log_step[0], e_b_re[0], e_b_im[0])
    bc, cc = _s5_block_weights(bb_re, bb_im, e_c_re[0], e_c_im[0])
    a_re = jnp.broadcast_to(ab_re.reshape(1, S5_NSTATE), (NBATCH, S5_NSTATE))
    a_im = jnp.broadcast_to(ab_im.reshape(1, S5_NSTATE), (NBATCH, S5_NSTATE))
    conv_w = jnp.broadcast_to(e_conv_w[0][:, None, :], (CONV_KERNEL, NBATCH, CONV_WIDTH))

    h1 = _layer0(x.reshape(NBATCH, seq * D_MODEL), row(norm_g[0]), e_w_in[0].astype(BF16), a_re, a_im, bc, cc,
                 row(e_d_skip[0]), e_w_glu[0].astype(BF16), row(e_b_glu[0]), conv_w, row(e_conv_b[0]),
                 row(e_ln_g[0]), row(e_ln_b[0]), e_w_out[0].astype(BF16), seq=seq, tt=tt)
    h1 = h1.reshape(NBATCH * seq, D_MODEL)

    cos_t, sin_t = _rope_tables(seq)
    q, k, qi, ki, wi, gate, v = _inproj1(h1, row(norm_g[1]), _reorder_odd_weights(o_w_in[0]).astype(BF16),
                                         cos_t, sin_t, seq=seq, rows=512)
    return _attention(q, qi, wi, gate, h1, k, v, ki, o_w_out[0].astype(BF16), row(final_g), seq=seq, ksel=ksel)
```

```python
import functools
import math

import numpy as np
import jax
import jax.numpy as jnp
from jax import lax
from jax.experimental import pallas as pl
from jax.experimental.pallas import tpu as pltpu

F32 = jnp.float32
BF16 = jnp.bfloat16
I32 = jnp.int32

NBATCH = 8
D_MODEL = 1024
CHUNK = 64
NORM_EPS = 1e-6
S5_WIDTH = 512
S5_GROUP = 16
S5_GROUPS = 32
S5_STATE = 64
S5_NSTATE = S5_GROUPS * S5_STATE
S5_SLABS = 4
CONV_WIDTH = 512
CONV_KERNEL = 31
CONV_HALO = (CONV_KERNEL - 1) * NBATCH
EVEN_IN = 2 * S5_WIDTH + 3 * CONV_WIDTH
N_HEADS = 16
HEAD_DIM = 64
N_KV_HEADS = 4
Q_PER_KV = 4
ATT_WIDTH = 1024
KV_WIDTH = 256
IDX_HEADS = 8
IDX_DIM = 64
TOPK_MAX = 256
ROPE_THETA = 500000.0
ROT_DIM = 16
LANES = 128
ODD_PAD = 3200
ATT_TILE = 256
PV_ROWS = HEAD_DIM + 16
BISECT_STEPS = 14
PEEL_STEPS = 6
INT_MIN = np.int32(-2 ** 31)
F32_MAX = float(np.finfo(np.float32).max)
NEG_BIG = -1e38
LOG2E = math.log2(math.e)

VMEM_LIMIT = 56 * 1024 * 1024


def _dot(a, b):
    return jnp.dot(a, b, preferred_element_type=F32)


def _dot_nt(a, b):
    return lax.dot_general(a, b, (((1,), (1,)), ((), ())), preferred_element_type=F32)


def _sigmoid(x):
    return 1.0 / (1.0 + jnp.exp(-x))


def _silu(x):
    return x * _sigmoid(x)


def _gelu_tanh(x):
    c = math.sqrt(2.0 / math.pi)
    return 0.5 * x * (1.0 + jnp.tanh(c * (x + 0.044715 * (x * x * x))))


def _s5_prep_kernel(lr_ref, li_ref, ls_ref, br_ref, bi_ref, abr_ref, abi_ref, bbr_ref, bbi_ref):
    lr = jnp.minimum(lr_ref[...], -1e-4)
    li = li_ref[...]
    dt = jnp.exp(ls_ref[...])
    mag = jnp.exp(lr * dt)
    ab_re = mag * jnp.cos(li * dt)
    ab_im = mag * jnp.sin(li * dt)
    den = lr * lr + li * li
    nr = ab_re - 1.0
    ni = ab_im
    k_re = (nr * lr + ni * li) / den
    k_im = (ni * lr - nr * li) / den
    br = br_ref[...]
    bi = bi_ref[...]
    abr_ref[...] = ab_re
    abi_ref[...] = ab_im
    bbr_ref[...] = k_re * br - k_im * bi
    bbi_ref[...] = k_re * bi + k_im * br


def _s5_prep(lam_re, lam_im, log_step, b_re, b_im):
    g, n, c = b_re.shape
    shape2 = (g * n * c // LANES, LANES)
    expand = lambda a: jnp.broadcast_to(a[:, :, None], (g, n, c)).reshape(shape2)
    ls = jnp.broadcast_to(log_step[:, None], (g, n))
    outs = pl.pallas_call(
        _s5_prep_kernel,
        out_shape=[jax.ShapeDtypeStruct(shape2, F32)] * 4,
        name="s5_prep",
    )(expand(lam_re), expand(lam_im), expand(ls), b_re.reshape(shape2), b_im.reshape(shape2))
    ab_re, ab_im, bb_re, bb_im = [o.reshape(g, n, c) for o in outs]
    return ab_re[:, :, 0], ab_im[:, :, 0], bb_re, bb_im


def _s5_block_weights(bb_re, bb_im, c_re, c_im):
    gl = S5_GROUPS // S5_SLABS
    eye = jnp.eye(gl, dtype=F32)

    def in_map(bb):
        t = bb.reshape(S5_SLABS, gl, S5_STATE, S5_GROUP).transpose(0, 1, 3, 2)
        t = t[:, :, :, None, :] * eye[None, :, None, :, None]
        return t.reshape(S5_SLABS, gl * S5_GROUP, gl * S5_STATE)

    def out_map(cc):
        t = cc.reshape(S5_SLABS, gl, S5_GROUP, S5_STATE).transpose(0, 1, 3, 2)
        t = t[:, :, :, None, :] * eye[None, :, None, :, None]
        return t.reshape(S5_SLABS, gl * S5_STATE, gl * S5_GROUP)

    bc = jnp.concatenate([in_map(bb_re), in_map(bb_im)], axis=2).astype(BF16)
    cc = jnp.concatenate([out_map(c_re), out_map(-c_im)], axis=1).astype(BF16)
    return bc, cc


def _layer0_kernel(x_ref, g_ref, win_ref, are_ref, aim_ref, bc_ref, cc_ref, dsk_ref, wglu_ref, bglu_ref,
                   cw_ref, cb_ref, lng_ref, lnb_ref, wout_ref, o_ref,
                   hn_ref, proj_ref, xs_ref, st_ref, hc_ref, gl_ref, ycat_ref, tm_ref, *, tt):
    rows = NBATCH * tt
    half = S5_NSTATE
    slab_w = half // S5_SLABS

    @pl.when(pl.program_id(0) == 0)
    def _():
        st_ref[...] = jnp.zeros_like(st_ref)
        hc_ref[0:CONV_HALO, :] = jnp.zeros((CONV_HALO, CONV_WIDTH), F32)

    gain = g_ref[...]
    for tp in range(tt // 2):
        xa = x_ref[:, (2 * tp) * D_MODEL:(2 * tp + 1) * D_MODEL]
        xb = x_ref[:, (2 * tp + 1) * D_MODEL:(2 * tp + 2) * D_MODEL]
        xx = jnp.concatenate([xa, xb], axis=0)
        ms = jnp.mean(xx * xx, axis=-1, keepdims=True)
        hn_ref[16 * tp:16 * tp + 16, :] = (xx * lax.rsqrt(ms + NORM_EPS) * gain).astype(BF16)

    proj_ref[...] = _dot(hn_ref[...], win_ref[...])

    for m in range(S5_SLABS):
        u_m = proj_ref[:, LANES * m:LANES * (m + 1)].astype(BF16)
        xm = _dot(u_m, bc_ref[m])
        xs_ref[:, slab_w * m:slab_w * (m + 1)] = xm[:, :slab_w]
        xs_ref[:, half + slab_w * m:half + slab_w * (m + 1)] = xm[:, slab_w:]

    cw = 512
    for cc_i in range(half // cw):
        lo = cc_i * cw
        a_re = are_ref[:, lo:lo + cw]
        a_im = aim_ref[:, lo:lo + cw]

        def step(t, carry, lo=lo, a_re=a_re, a_im=a_im):
            s_re, s_im = carry
            r0 = pl.multiple_of(t * NBATCH, NBATCH)
            x_re = xs_ref[pl.ds(r0, NBATCH), lo:lo + cw]
            x_im = xs_ref[pl.ds(r0, NBATCH), half + lo:half + lo + cw]
            n_re = a_re * s_re - a_im * s_im + x_re
            n_im = a_re * s_im + a_im * s_re + x_im
            xs_ref[pl.ds(r0, NBATCH), lo:lo + cw] = n_re
            xs_ref[pl.ds(r0, NBATCH), half + lo:half + lo + cw] = n_im
            return n_re, n_im

        s_re, s_im = lax.fori_loop(0, tt, step, (st_ref[:, lo:lo + cw], st_ref[:, half + lo:half + lo + cw]),
                                   unroll=8)
        st_ref[:, lo:lo + cw] = s_re
        st_ref[:, half + lo:half + lo + cw] = s_im

    for m in range(S5_SLABS):
        s_re = xs_ref[:, slab_w * m:slab_w * (m + 1)].astype(BF16)
        s_im = xs_ref[:, half + slab_w * m:half + slab_w * (m + 1)].astype(BF16)
        y_m = _dot(s_re, cc_ref[m, 0:slab_w, :]) + _dot(s_im, cc_ref[m, slab_w:2 * slab_w, :])
        y_m = y_m + proj_ref[:, LANES * m:LANES * (m + 1)] * dsk_ref[:, LANES * m:LANES * (m + 1)]
        gl_ref[:, LANES * m:LANES * (m + 1)] = _gelu_tanh(y_m)
    gl = gl_ref[...]
    out_a = gl * _sigmoid(_dot(gl.astype(BF16), wglu_ref[...]) + bglu_ref[...])
    ycat_ref[:, 0:S5_WIDTH] = (out_a * _silu(proj_ref[:, S5_WIDTH:2 * S5_WIDTH])).astype(BF16)

    o = 2 * S5_WIDTH
    hc_ref[CONV_HALO:CONV_HALO + rows, :] = (
        proj_ref[:, o:o + CONV_WIDTH] * _sigmoid(proj_ref[:, o + CONV_WIDTH:o + 2 * CONV_WIDTH]))
    rc = 64
    for cl in range(CONV_WIDTH // LANES):
        lanes = slice(LANES * cl, LANES * (cl + 1))
        taps = [cw_ref[k, :, lanes] for k in range(CONV_KERNEL)]

        def conv_chunk(c, _, lanes=lanes, taps=taps):
            r0 = pl.multiple_of(c * rc, rc)
            accs = [None] * (rc // NBATCH)
            for t in range(rc // NBATCH + CONV_KERNEL - 1):
                h = hc_ref[pl.ds(r0 + NBATCH * t, NBATCH), lanes]
                for rr in range(rc // NBATCH):
                    k = t - rr
                    if 0 <= k < CONV_KERNEL:
                        accs[rr] = h * taps[k] if accs[rr] is None else accs[rr] + h * taps[k]
            gl_ref[pl.ds(r0, rc), lanes] = jnp.concatenate(accs, axis=0)
            return 0

        lax.fori_loop(0, rows // rc, conv_chunk, 0)
    hc_ref[0:CONV_HALO, :] = hc_ref[rows:rows + CONV_HALO, :]
    h = gl_ref[...] + cb_ref[...]
    mu = jnp.mean(h, axis=-1, keepdims=True)
    var = jnp.mean(jnp.square(h - mu), axis=-1, keepdims=True)
    hf = (h - mu) * lax.rsqrt(var + NORM_EPS) * lng_ref[...] + lnb_ref[...]
    z_b = proj_ref[:, o + 2 * CONV_WIDTH:o + 3 * CONV_WIDTH]
    ycat_ref[:, S5_WIDTH:S5_WIDTH + CONV_WIDTH] = (_silu(hf) * _silu(z_b)).astype(BF16)

    y = _dot(ycat_ref[...], wout_ref[...])
    for t in range(tt):
        h = y[NBATCH * t:NBATCH * (t + 1), :] + x_ref[:, t * D_MODEL:(t + 1) * D_MODEL]
        for c in range(D_MODEL // LANES):
            tm_ref[c, NBATCH * t:NBATCH * (t + 1), :] = h[:, LANES * c:LANES * (c + 1)]
    for b in range(NBATCH):
        for c in range(D_MODEL // LANES):
            o_ref[b, :, LANES * c:LANES * (c + 1)] = tm_ref[c, pl.ds(b, tt, stride=NBATCH), :]


def _layer0(x2, gain, w_in, a_re, a_im, bc, cc, d_skip, w_glu, b_glu, conv_w, conv_b, ln_g, ln_b, w_out, *, seq, tt):
    rows = NBATCH * tt
    const = lambda shape: pl.BlockSpec(shape, lambda i: (0,) * len(shape))
    return pl.pallas_call(
        functools.partial(_layer0_kernel, tt=tt),
        grid=(seq // tt,),
        in_specs=[
            pl.BlockSpec((NBATCH, tt * D_MODEL), lambda i: (0, i)),
            const((1, D_MODEL)),
            const((D_MODEL, EVEN_IN)),
            const((NBATCH, S5_NSTATE)),
            const((NBATCH, S5_NSTATE)),
            const((S5_SLABS, LANES, 2 * S5_NSTATE // S5_SLABS)),
            const((S5_SLABS, 2 * S5_NSTATE // S5_SLABS, LANES)),
            const((1, S5_WIDTH)),
            const((S5_WIDTH, S5_WIDTH)),
            const((1, S5_WIDTH)),
            const((CONV_KERNEL, NBATCH, CONV_WIDTH)),
            const((1, CONV_WIDTH)),
            const((1, CONV_WIDTH)),
            const((1, CONV_WIDTH)),
            const((S5_WIDTH + CONV_WIDTH, D_MODEL)),
        ],
        out_specs=pl.BlockSpec((NBATCH, tt, D_MODEL), lambda i: (0, i, 0)),
        out_shape=jax.ShapeDtypeStruct((NBATCH, seq, D_MODEL), F32),
        scratch_shapes=[
            pltpu.VMEM((rows, D_MODEL), BF16),
            pltpu.VMEM((rows, EVEN_IN), F32),
            pltpu.VMEM((rows, 2 * S5_NSTATE), F32),
            pltpu.VMEM((NBATCH, 2 * S5_NSTATE), F32),
            pltpu.VMEM((CONV_HALO + rows, CONV_WIDTH), F32),
            pltpu.VMEM((rows, S5_WIDTH), F32),
            pltpu.VMEM((rows, S5_WIDTH + CONV_WIDTH), BF16),
            pltpu.VMEM((D_MODEL // LANES, rows, LANES), F32),
        ],
        compiler_params=pltpu.CompilerParams(dimension_semantics=("arbitrary",), vmem_limit_bytes=VMEM_LIMIT),
        name="layer0",
    )(x2, gain, w_in, a_re, a_im, bc, cc, d_skip, w_glu, b_glu, conv_w, conv_b, ln_g, ln_b, w_out)


def _inproj1_kernel(h_ref, g_ref, w_ref, cos_ref, sin_ref,
                    q_ref, k_ref, qi_ref, ki_ref, wi_ref, gate_ref, v_ref):
    h = h_ref[...]
    ms = jnp.mean(h * h, axis=-1, keepdims=True)
    hn = (h * lax.rsqrt(ms + NORM_EPS) * g_ref[...]).astype(BF16)
    proj = _dot(hn, w_ref[...])
    cos_t = cos_ref[...]
    sin_t = sin_ref[...]
    lane = lax.broadcasted_iota(I32, cos_t.shape, 1)
    first = (lane & (HEAD_DIM - 1)) < (ROT_DIM // 2)
    low = lane < HEAD_DIM

    def slab(c):
        return proj[:, LANES * c:LANES * (c + 1)]

    def rope(xs, ct, st):
        partner = jnp.where(first, pltpu.roll(xs, LANES - ROT_DIM // 2, 1), pltpu.roll(xs, ROT_DIM // 2, 1))
        return xs * ct + partner * st

    q_scale = (HEAD_DIM ** -0.5) * LOG2E
    for c in range(8):
        q_ref[:, LANES * c:LANES * (c + 1)] = (rope(slab(c), cos_t, sin_t) * q_scale).astype(BF16)
    for c in range(2):
        k_ref[:, LANES * c:LANES * (c + 1)] = rope(slab(8 + c), cos_t, sin_t).astype(BF16)
    for c in range(4):
        qi_ref[:, LANES * c:LANES * (c + 1)] = rope(slab(10 + c), cos_t, sin_t).astype(BF16)
    kiw = rope(slab(14), jnp.where(low, cos_t, 1.0), jnp.where(low, sin_t, 0.0))
    ki_ref[...] = kiw.astype(BF16)
    wi_ref[...] = kiw
    for c in range(8):
        gate_ref[:, LANES * c:LANES * (c + 1)] = _silu(slab(15 + c)).astype(BF16)
    for c in range(2):
        v_ref[:, LANES * c:LANES * (c + 1)] = slab(23 + c).astype(BF16)


def _inproj1(h1, gain, w, cos_t, sin_t, *, seq, rows):
    n = seq * NBATCH
    const = lambda shape: pl.BlockSpec(shape, lambda i: (0,) * len(shape))
    rb = lambda w_: pl.BlockSpec((rows, w_), lambda i: (i, 0))
    pos = pl.BlockSpec((rows, LANES), lambda i: (i % (seq // rows), 0))
    return pl.pallas_call(
        _inproj1_kernel,
        grid=(n // rows,),
        in_specs=[rb(D_MODEL), const((1, D_MODEL)), const((D_MODEL, ODD_PAD)), pos, pos],
        out_specs=[rb(ATT_WIDTH), rb(KV_WIDTH), rb(IDX_HEADS * IDX_DIM), rb(LANES), rb(LANES), rb(ATT_WIDTH),
                   rb(KV_WIDTH)],
        out_shape=[
            jax.ShapeDtypeStruct((n, ATT_WIDTH), BF16),
            jax.ShapeDtypeStruct((n, KV_WIDTH), BF16),
            jax.ShapeDtypeStruct((n, IDX_HEADS * IDX_DIM), BF16),
            jax.ShapeDtypeStruct((n, LANES), BF16),
            jax.ShapeDtypeStruct((n, LANES), F32),
            jax.ShapeDtypeStruct((n, ATT_WIDTH), BF16),
            jax.ShapeDtypeStruct((n, KV_WIDTH), BF16),
        ],
        compiler_params=pltpu.CompilerParams(dimension_semantics=("arbitrary",), vmem_limit_bytes=VMEM_LIMIT),
        name="inproj1",
    )(h1, gain, w, cos_t, sin_t)


def _attn_kernel(q_ref, qi_ref, wi_ref, gate_ref, h1_ref, k_ref, v_ref, ki_ref, wout_ref, fg_ref, o_ref,
                 vt_ref, sc_ref, qh_ref, qih_ref, acc_ref, m_ref, jcut_ref, s_ref, mb_ref, thr_ref, cnt_ref,
                 *, seq, ksel):
    tq = ATT_TILE
    i = pl.program_id(1)
    nblk = i + 1

    @pl.when(i == 0)
    def _():
        def transpose_v(kb, _):
            r0 = pl.multiple_of(kb * tq, tq)
            vt = v_ref[pl.ds(r0, tq), :].astype(F32).T
            for j in range(N_KV_HEADS):
                vt_ref[kb, j, 0:HEAD_DIM, :] = vt[HEAD_DIM * j:HEAD_DIM * (j + 1), :].astype(BF16)
                vt_ref[kb, j, HEAD_DIM:PV_ROWS, :] = jnp.ones((PV_ROWS - HEAD_DIM, tq), BF16)
            return 0
        lax.fori_loop(0, seq // tq, transpose_v, 0)

    zero_half = jnp.zeros((HEAD_DIM, tq), BF16)
    for c in range(ATT_WIDTH // LANES):
        t = q_ref[:, LANES * c:LANES * (c + 1)].astype(F32).T
        for half in range(2):
            j, g = divmod(2 * c + half, Q_PER_KV)
            lo = (j % 2) * HEAD_DIM
            cols = slice(tq * g, tq * (g + 1))
            qh_ref[j, lo:lo + HEAD_DIM, cols] = t[HEAD_DIM * half:HEAD_DIM * (half + 1), :].astype(BF16)
            qh_ref[j, HEAD_DIM - lo:2 * HEAD_DIM - lo, cols] = zero_half
    for c in range(IDX_HEADS * IDX_DIM // LANES):
        t = qi_ref[:, LANES * c:LANES * (c + 1)].astype(F32).T
        for half in range(2):
            h = 2 * c + half
            qih_ref[0:IDX_DIM, tq * h:tq * (h + 1)] = t[IDX_DIM * half:IDX_DIM * (half + 1), :].astype(BF16)
    qih_ref[IDX_DIM:LANES, :] = jnp.zeros((LANES - IDX_DIM, IDX_HEADS * tq), BF16)
    w8 = wi_ref[...].T[IDX_DIM:IDX_DIM + IDX_HEADS, :] * (IDX_HEADS ** -0.5) * (IDX_DIM ** -0.5)

    row = lax.broadcasted_iota(I32, (tq, tq), 0)
    col = lax.broadcasted_iota(I32, (tq, tq), 1)
    row_chunk = row >> 6
    col_chunk = col >> 6

    def tree(op, x):
        parts = [x[8 * r:8 * (r + 1)] for r in range(x.shape[0] // 8)]
        while len(parts) > 1:
            parts = [op(parts[a], parts[a + 1]) for a in range(0, len(parts), 2)]
        return parts[0]

    def score_block(kb, carry):
        smax, smin = carry
        r0 = pl.multiple_of(kb * tq, tq)
        s = _dot(ki_ref[pl.ds(r0, tq), :], qih_ref[...])
        acc = jnp.maximum(s[:, 0:tq], 0.0) * w8[0:1, :]
        for h in range(1, IDX_HEADS):
            acc = acc + jnp.maximum(s[:, tq * h:tq * (h + 1)], 0.0) * w8[h:h + 1, :]
        acc = jnp.where(acc == 0.0, 0.0, acc)
        later = ((kb - i) * (tq // CHUNK) + row_chunk) > col_chunk
        sc_ref[pl.ds(r0, tq), :] = jnp.where(later, -jnp.inf, acc)
        smax = jnp.maximum(smax, tree(jnp.maximum, jnp.where(later, -jnp.inf, acc)))
        smin = jnp.minimum(smin, tree(jnp.minimum, jnp.where(later, jnp.inf, acc)))
        return smax, smin

    smax, smin = lax.fori_loop(0, nblk, score_block,
                               (jnp.full((8, tq), -jnp.inf, F32), jnp.full((8, tq), jnp.inf, F32)))
    smax = jnp.max(smax, axis=0, keepdims=True)
    smin = jnp.min(smin, axis=0, keepdims=True)

    def lane_pass(elem, op, init):
        def body(kb, acc):
            r0 = pl.multiple_of(kb * tq, tq)
            return op(acc, tree(op, elem(sc_ref[pl.ds(r0, tq), :], r0)))
        return lax.fori_loop(0, nblk, body, init)

    def count(pred):
        acc = lane_pass(lambda blk, r0: jnp.where(pred(blk, r0), 1, 0).astype(I32), jnp.add,
                        jnp.zeros((8, tq), I32))
        return jnp.sum(acc, axis=0, keepdims=True)

    def max_below(bound):
        acc = lane_pass(lambda blk, r0: jnp.where(blk < bound, blk, -jnp.inf), jnp.maximum,
                        jnp.full((8, tq), -jnp.inf, F32))
        return jnp.max(acc, axis=0, keepdims=True)

    def as_threshold(key):
        ks = key ^ INT_MIN
        return pltpu.bitcast(jnp.where(ks >= 0, ks, ks ^ np.int32(0x7FFFFFFF)), F32)

    def next_up(x):
        bits = pltpu.bitcast(x, I32)
        ks = jnp.where(bits >= 0, bits, bits ^ np.int32(0x7FFFFFFF)) + 1
        return pltpu.bitcast(jnp.where(ks >= 0, ks, ks ^ np.int32(0x7FFFFFFF)), F32)

    n_adm = (CHUNK * (tq // CHUNK) * i + CHUNK) + CHUNK * col_chunk[0:1, :]
    few = n_adm <= ksel

    def halve(_, st):
        lo, hi, c_lo, c_hi = st
        t = 0.5 * lo + 0.5 * hi
        inside = (t > lo) & (t < hi)
        t = jnp.where(inside, t, lo)
        c = count(lambda blk, r0: blk >= t)
        up = inside & (c >= ksel)
        dn = inside & (c < ksel)
        return jnp.where(up, t, lo), jnp.where(dn, t, hi), jnp.where(up, c, c_lo), jnp.where(dn, c, c_hi)

    zero = jnp.zeros((1, tq), I32)
    lo, hi, c_lo, c_hi = lax.fori_loop(
        0, BISECT_STEPS, halve, (smin, next_up(jnp.minimum(smax, F32_MAX)), n_adm, zero))

    def unresolved(c_lo, exact):
        return (c_lo > ksel) & (exact == 0) & jnp.logical_not(few)

    def peel_cond(st):
        it, lo, hi, c_lo, c_hi, exact = st
        return (it < PEEL_STEPS) & (jnp.max(jnp.where(unresolved(c_lo, exact), 1, 0)) > 0)

    def peel(st):
        it, lo, hi, c_lo, c_hi, exact = st
        v = max_below(hi)
        c_v = count(lambda blk, r0: blk >= v)
        act = unresolved(c_lo, exact)
        hit = act & (c_v >= ksel)
        miss = act & (c_v < ksel)
        return (it + 1, jnp.where(hit, v, lo), jnp.where(miss, v, hi), jnp.where(hit, c_v, c_lo),
                jnp.where(miss, c_v, c_hi), jnp.where(hit, 1, exact))

    _, lo, hi, c_lo, c_hi, exact = lax.while_loop(peel_cond, peel, (jnp.int32(0), lo, hi, c_lo, c_hi, zero))
    thr_ref[...] = jnp.where(few, -F32_MAX, lo)
    cnt_ref[0:1, :] = c_lo
    cnt_ref[1:2, :] = c_hi

    @pl.when(jnp.max(jnp.where(unresolved(c_lo, exact), 1, 0)) > 0)
    def _():
        def bisect(it, carry):
            t, cnt_t = carry
            cand = t | (jnp.int32(1) << (31 - it))
            cand_f = as_threshold(cand)
            c = count(lambda blk, r0: blk >= cand_f)
            ok = c >= ksel
            return jnp.where(ok, cand, t), jnp.where(ok, c, cnt_t)

        thr_key, cnt_ge = lax.fori_loop(0, 32, bisect, (zero, zero))
        t = as_threshold(thr_key)
        t = jnp.maximum(jnp.where(t != t, -F32_MAX, t), -F32_MAX)
        thr_ref[...] = t
        cnt_ref[0:1, :] = cnt_ge
        cnt_ref[1:2, :] = count(lambda blk, r0: blk > t)

    thr = thr_ref[...]
    cnt_ge = cnt_ref[0:1, :]
    need = ksel - cnt_ref[1:2, :]

    jbits = int(seq).bit_length()
    jcut_ref[...] = jnp.full((1, tq), (1 << jbits) - 1, I32)

    @pl.when(jnp.max(jnp.where(cnt_ge > ksel, 1, 0)) > 0)
    def _():
        def jbisect(it, jc):
            cand = jc | (jnp.int32(1) << (jbits - 1 - it))
            f = count(lambda blk, r0: (blk == thr) & ((r0 + row) < cand))
            return jnp.where(f <= need, cand, jc)
        jcut_ref[...] = lax.fori_loop(0, jbits, jbisect, jnp.zeros((1, tq), I32))

    jcut = jcut_ref[...]

    m_ref[...] = jnp.full(m_ref.shape, NEG_BIG, F32)
    acc_ref[...] = jnp.zeros(acc_ref.shape, F32)

    def logits(kb, slot):
        r0 = pl.multiple_of(kb * tq, tq)
        blk = sc_ref[pl.ds(r0, tq), :]
        sel = (blk > thr) | ((blk == thr) & ((r0 + row) < jcut))
        bias = jnp.where(sel, 0.0, -jnp.inf)
        bias = jnp.concatenate([bias] * Q_PER_KV, axis=1)
        for j in range(N_KV_HEADS):
            s = _dot(k_ref[pl.ds(r0, tq), LANES * (j // 2):LANES * (j // 2 + 1)], qh_ref[j]) + bias
            s_ref[slot, j] = s
            mb_ref[slot, j:j + 1, :] = jnp.max(s, axis=0, keepdims=True)

    def accumulate(kb, slot):
        for j in range(N_KV_HEADS):
            m_old = m_ref[j:j + 1, :]
            m_new = jnp.maximum(m_old, mb_ref[slot, j:j + 1, :])
            m_ref[j:j + 1, :] = m_new
            p = jnp.exp2(s_ref[slot, j] - m_new).astype(BF16)
            acc_ref[j] = acc_ref[j] * jnp.exp2(m_old - m_new) + _dot(vt_ref[kb, j], p)

    def attend_pair(t, _):
        kb = 2 * t
        logits(kb + 1, 1)
        accumulate(kb, 0)
        logits(kb + 2, 0)
        accumulate(kb + 1, 1)
        return 0

    logits(0, 0)
    npair = (nblk - 1) // 2
    lax.fori_loop(0, npair, attend_pair, 0)
    done = 2 * npair

    @pl.when(nblk - done == 1)
    def _():
        accumulate(done, 0)

    @pl.when(nblk - done == 2)
    def _():
        logits(done + 1, 1)
        accumulate(done, 0)
        accumulate(done + 1, 1)

    slabs = []
    for pair in range(N_HEADS // 2):
        j, g0 = pair // 2, 2 * (pair % 2)
        inv_l = 1.0 / acc_ref[j, HEAD_DIM:HEAD_DIM + 1, tq * g0:tq * (g0 + 2)]
        o2 = acc_ref[j, 0:HEAD_DIM, tq * g0:tq * (g0 + 2)] * inv_l
        slabs.append(jnp.concatenate([o2[:, 0:tq], o2[:, tq:2 * tq]], axis=0).T)
    att = jnp.concatenate(slabs, axis=1)
    y = _dot((att * gate_ref[...].astype(F32)).astype(BF16), wout_ref[...])
    hh = h1_ref[...] + y
    ms = jnp.mean(hh * hh, axis=-1, keepdims=True)
    o_ref[0] = hh * lax.rsqrt(ms + NORM_EPS) * fg_ref[...]


def _attention(q, qi, wi, gate, h1, k, v, ki, w_out, final_g, *, seq, ksel):
    tq = ATT_TILE
    ntile = seq // tq
    tile = lambda w_: pl.BlockSpec((tq, w_), lambda b, i: (b * ntile + i, 0))
    whole = lambda w_: pl.BlockSpec((seq, w_), lambda b, i: (b, 0))
    const = lambda shape: pl.BlockSpec(shape, lambda b, i: (0,) * len(shape))
    return pl.pallas_call(
        functools.partial(_attn_kernel, seq=seq, ksel=ksel),
        grid=(NBATCH, seq // tq),
        in_specs=[tile(ATT_WIDTH), tile(IDX_HEADS * IDX_DIM), tile(LANES), tile(ATT_WIDTH), tile(D_MODEL),
                  whole(KV_WIDTH), whole(KV_WIDTH), whole(LANES),
                  const((ATT_WIDTH, D_MODEL)), const((1, D_MODEL))],
        out_specs=pl.BlockSpec((1, tq, D_MODEL), lambda b, i: (b, i, 0)),
        out_shape=jax.ShapeDtypeStruct((NBATCH, seq, D_MODEL), F32),
        scratch_shapes=[
            pltpu.VMEM((seq // tq, N_KV_HEADS, PV_ROWS, tq), BF16),
            pltpu.VMEM((seq, tq), F32),
            pltpu.VMEM((N_KV_HEADS, LANES, Q_PER_KV * tq), BF16),
            pltpu.VMEM((LANES, IDX_HEADS * tq), BF16),
            pltpu.VMEM((N_KV_HEADS, PV_ROWS, Q_PER_KV * tq), F32),
            pltpu.VMEM((N_KV_HEADS, Q_PER_KV * tq), F32),
            pltpu.VMEM((1, tq), I32),
            pltpu.VMEM((2, N_KV_HEADS, tq, Q_PER_KV * tq), F32),
            pltpu.VMEM((2, N_KV_HEADS, Q_PER_KV * tq), F32),
            pltpu.VMEM((1, tq), F32),
            pltpu.VMEM((2, tq), I32),
        ],
        compiler_params=pltpu.CompilerParams(dimension_semantics=("arbitrary", "arbitrary"),
                                             vmem_limit_bytes=VMEM_LIMIT),
        name="dsa_attention",
    )(q, qi, wi, gate, h1, k, v, ki, w_out, final_g)


def _rope_tables(seq):
    pos = jnp.arange(seq, dtype=F32)
    inv = ROPE_THETA ** (-jnp.arange(0, ROT_DIM, 2, dtype=F32) / ROT_DIM)
    ang = pos[:, None] * inv[None, :]
    cos, sin = jnp.cos(ang), jnp.sin(ang)
    ones = jnp.ones((seq, HEAD_DIM - ROT_DIM), F32)
    cos_h = jnp.concatenate([cos, cos, ones], axis=1)
    sin_h = jnp.concatenate([-sin, sin, 0.0 * ones], axis=1)
    return jnp.concatenate([cos_h, cos_h], axis=1), jnp.concatenate([sin_h, sin_h], axis=1)


def _reorder_odd_weights(w):
    o = np.cumsum([0, ATT_WIDTH, ATT_WIDTH, KV_WIDTH, KV_WIDTH, IDX_HEADS * IDX_DIM, IDX_DIM, IDX_HEADS]).tolist()
    q, z, k, v, qi, ki, wi = [w[:, o[j]:o[j + 1]] for j in range(7)]
    pad = jnp.zeros((w.shape[0], LANES - IDX_DIM - IDX_HEADS), w.dtype)
    return jnp.concatenate([q, k, qi, ki, wi, pad, z, v], axis=1)


def kernel(x, norm_g, e_w_in, e_lam_re, e_lam_im, e_log_step, e_b_re, e_b_im, e_c_re, e_c_im, e_d_skip,
           e_w_glu, e_b_glu, e_conv_w, e_conv_b, e_ln_g, e_ln_b, e_w_out, o_w_in, o_w_out, final_g):
    bsz, seq, dm = x.shape
    assert bsz == NBATCH and dm == D_MODEL and seq % ATT_TILE == 0
    assert norm_g.shape[0] == 2 and e_w_in.shape[0] == 1 and o_w_in.shape[0] == 1
    tt = 64
    ksel = min(TOPK_MAX, seq // 4)
    row = lambda a: a.reshape(1, -1).astype(F32)

    ab_re, ab_im, bb_re, bb_im = _s5_prep(e_lam_re[0], e_lam_im[0], e_log_step[0], e_b_re[0], e_b_im[0])
    bc, cc = _s5_block_weights(bb_re, bb_im, e_c_re[0], e_c_im[0])
    a_re = jnp.broadcast_to(ab_re.reshape(1, S5_NSTATE), (NBATCH, S5_NSTATE))
    a_im = jnp.broadcast_to(ab_im.reshape(1, S5_NSTATE), (NBATCH, S5_NSTATE))
    conv_w = jnp.broadcast_to(e_conv_w[0][:, None, :], (CONV_KERNEL, NBATCH, CONV_WIDTH))

    h1 = _layer0(x.reshape(NBATCH, seq * D_MODEL), row(norm_g[0]), e_w_in[0].astype(BF16), a_re, a_im, bc, cc,
                 row(e_d_skip[0]), e_w_glu[0].astype(BF16), row(e_b_glu[0]), conv_w, row(e_conv_b[0]),
                 row(e_ln_g[0]), row(e_ln_b[0]), e_w_out[0].astype(BF16), seq=seq, tt=tt)
    h1 = h1.reshape(NBATCH * seq, D_MODEL)

    cos_t, sin_t = _rope_tables(seq)
    q, k, qi, ki, wi, gate, v = _inproj1(h1, row(norm_g[1]), _reorder_odd_weights(o_w_in[0]).astype(BF16),
                                         cos_t, sin_t, seq=seq, rows=512)
    return _attention(q, qi, wi, gate, h1, k, v, ki, o_w_out[0].astype(BF16), row(final_g), seq=seq, ksel=ksel)
```

```python
import functools
import math

import numpy as np
import jax
import jax.numpy as jnp
from jax import lax
from jax.experimental import pallas as pl
from jax.experimental.pallas import tpu as pltpu

F32 = jnp.float32
BF16 = jnp.bfloat16
I32 = jnp.int32

NBATCH = 8
D_MODEL = 1024
CHUNK = 64
NORM_EPS = 1e-6
S5_WIDTH = 512
S5_GROUP = 16
S5_GROUPS = 32
S5_STATE = 64
S5_NSTATE = S5_GROUPS * S5_STATE
S5_SLABS = 4
CONV_WIDTH = 512
CONV_KERNEL = 31
CONV_HALO = (CONV_KERNEL - 1) * NBATCH
EVEN_IN = 2 * S5_WIDTH + 3 * CONV_WIDTH
N_HEADS = 16
HEAD_DIM = 64
N_KV_HEADS = 4
Q_PER_KV = 4
ATT_WIDTH = 1024
KV_WIDTH = 256
IDX_HEADS = 8
IDX_DIM = 64
TOPK_MAX = 256
ROPE_THETA = 500000.0
ROT_DIM = 16
LANES = 128
ODD_PAD = 3200
ATT_TILE = 256
PV_ROWS = HEAD_DIM + 16
BISECT_STEPS = 16
PEEL_STEPS = 6
INT_MIN = np.int32(-2 ** 31)
F32_MAX = float(np.finfo(np.float32).max)
NEG_BIG = -1e38
LOG2E = math.log2(math.e)

VMEM_LIMIT = 56 * 1024 * 1024


def _dot(a, b):
    return jnp.dot(a, b, preferred_element_type=F32)


def _dot_nt(a, b):
    return lax.dot_general(a, b, (((1,), (1,)), ((), ())), preferred_element_type=F32)


def _sigmoid(x):
    return 1.0 / (1.0 + jnp.exp(-x))


def _silu(x):
    return x * _sigmoid(x)


def _gelu_tanh(x):
    c = math.sqrt(2.0 / math.pi)
    return 0.5 * x * (1.0 + jnp.tanh(c * (x + 0.044715 * (x * x * x))))


def _s5_prep_kernel(lr_ref, li_ref, ls_ref, br_ref, bi_ref, abr_ref, abi_ref, bbr_ref, bbi_ref):
    lr = jnp.minimum(lr_ref[...], -1e-4)
    li = li_ref[...]
    dt = jnp.exp(ls_ref[...])
    mag = jnp.exp(lr * dt)
    ab_re = mag * jnp.cos(li * dt)
    ab_im = mag * jnp.sin(li * dt)
    den = lr * lr + li * li
    nr = ab_re - 1.0
    ni = ab_im
    k_re = (nr * lr + ni * li) / den
    k_im = (ni * lr - nr * li) / den
    br = br_ref[...]
    bi = bi_ref[...]
    abr_ref[...] = ab_re
    abi_ref[...] = ab_im
    bbr_ref[...] = k_re * br - k_im * bi
    bbi_ref[...] = k_re * bi + k_im * br


def _s5_prep(lam_re, lam_im, log_step, b_re, b_im):
    g, n, c = b_re.shape
    shape2 = (g * n * c // LANES, LANES)
    expand = lambda a: jnp.broadcast_to(a[:, :, None], (g, n, c)).reshape(shape2)
    ls = jnp.broadcast_to(log_step[:, None], (g, n))
    outs = pl.pallas_call(
        _s5_prep_kernel,
        out_shape=[jax.ShapeDtypeStruct(shape2, F32)] * 4,
        name="s5_prep",
    )(expand(lam_re), expand(lam_im), expand(ls), b_re.reshape(shape2), b_im.reshape(shape2))
    ab_re, ab_im, bb_re, bb_im = [o.reshape(g, n, c) for o in outs]
    return ab_re[:, :, 0], ab_im[:, :, 0], bb_re, bb_im


def _s5_block_weights(bb_re, bb_im, c_re, c_im):
    gl = S5_GROUPS // S5_SLABS
    eye = jnp.eye(gl, dtype=F32)

    def in_map(bb):
        t = bb.reshape(S5_SLABS, gl, S5_STATE, S5_GROUP).transpose(0, 1, 3, 2)
        t = t[:, :, :, None, :] * eye[None, :, None, :, None]
        return t.reshape(S5_SLABS, gl * S5_GROUP, gl * S5_STATE)

    def out_map(cc):
        t = cc.reshape(S5_SLABS, gl, S5_GROUP, S5_STATE).transpose(0, 1, 3, 2)
        t = t[:, :, :, None, :] * eye[None, :, None, :, None]
        return t.reshape(S5_SLABS, gl * S5_STATE, gl * S5_GROUP)

    bc = jnp.concatenate([in_map(bb_re), in_map(bb_im)], axis=2).astype(BF16)
    cc = jnp.concatenate([out_map(c_re), out_map(-c_im)], axis=1).astype(BF16)
    return bc, cc


def _layer0_kernel(x_ref, g_ref, win_ref, are_ref, aim_ref, bc_ref, cc_ref, dsk_ref, wglu_ref, bglu_ref,
                   cw_ref, cb_ref, lng_ref, lnb_ref, wout_ref, o_ref,
                   hn_ref, proj_ref, xs_ref, st_ref, hc_ref, gl_ref, ycat_ref, tm_ref, *, tt):
    rows = NBATCH * tt
    half = S5_NSTATE
    slab_w = half // S5_SLABS

    @pl.when(pl.program_id(0) == 0)
    def _():
        st_ref[...] = jnp.zeros_like(st_ref)
        hc_ref[0:CONV_HALO, :] = jnp.zeros((CONV_HALO, CONV_WIDTH), F32)

    gain = g_ref[...]
    for tp in range(tt // 2):
        xa = x_ref[:, (2 * tp) * D_MODEL:(2 * tp + 1) * D_MODEL]
        xb = x_ref[:, (2 * tp + 1) * D_MODEL:(2 * tp + 2) * D_MODEL]
        xx = jnp.concatenate([xa, xb], axis=0)
        ms = jnp.mean(xx * xx, axis=-1, keepdims=True)
        hn_ref[16 * tp:16 * tp + 16, :] = (xx * lax.rsqrt(ms + NORM_EPS) * gain).astype(BF16)

    proj_ref[...] = _dot(hn_ref[...], win_ref[...])

    for m in range(S5_SLABS):
        u_m = proj_ref[:, LANES * m:LANES * (m + 1)].astype(BF16)
        xm = _dot(u_m, bc_ref[m])
        xs_ref[:, slab_w * m:slab_w * (m + 1)] = xm[:, :slab_w]
        xs_ref[:, half + slab_w * m:half + slab_w * (m + 1)] = xm[:, slab_w:]

    cw = 512
    for cc_i in range(half // cw):
        lo = cc_i * cw
        a_re = are_ref[:, lo:lo + cw]
        a_im = aim_ref[:, lo:lo + cw]

        def step(t, carry, lo=lo, a_re=a_re, a_im=a_im):
            s_re, s_im = carry
            r0 = pl.multiple_of(t * NBATCH, NBATCH)
            x_re = xs_ref[pl.ds(r0, NBATCH), lo:lo + cw]
            x_im = xs_ref[pl.ds(r0, NBATCH), half + lo:half + lo + cw]
            n_re = a_re * s_re - a_im * s_im + x_re
            n_im = a_re * s_im + a_im * s_re + x_im
            xs_ref[pl.ds(r0, NBATCH), lo:lo + cw] = n_re
            xs_ref[pl.ds(r0, NBATCH), half + lo:half + lo + cw] = n_im
            return n_re, n_im

        s_re, s_im = lax.fori_loop(0, tt, step, (st_ref[:, lo:lo + cw], st_ref[:, half + lo:half + lo + cw]),
                                   unroll=True)
        st_ref[:, lo:lo + cw] = s_re
        st_ref[:, half + lo:half + lo + cw] = s_im

    for m in range(S5_SLABS):
        s_re = xs_ref[:, slab_w * m:slab_w * (m + 1)].astype(BF16)
        s_im = xs_ref[:, half + slab_w * m:half + slab_w * (m + 1)].astype(BF16)
        y_m = _dot(s_re, cc_ref[m, 0:slab_w, :]) + _dot(s_im, cc_ref[m, slab_w:2 * slab_w, :])
        y_m = y_m + proj_ref[:, LANES * m:LANES * (m + 1)] * dsk_ref[:, LANES * m:LANES * (m + 1)]
        gl_ref[:, LANES * m:LANES * (m + 1)] = _gelu_tanh(y_m)
    gl = gl_ref[...]
    out_a = gl * _sigmoid(_dot(gl.astype(BF16), wglu_ref[...]) + bglu_ref[...])
    ycat_ref[:, 0:S5_WIDTH] = (out_a * _silu(proj_ref[:, S5_WIDTH:2 * S5_WIDTH])).astype(BF16)

    o = 2 * S5_WIDTH
    hc_ref[CONV_HALO:CONV_HALO + rows, :] = (
        proj_ref[:, o:o + CONV_WIDTH] * _sigmoid(proj_ref[:, o + CONV_WIDTH:o + 2 * CONV_WIDTH]))
    rc = 64
    for cl in range(CONV_WIDTH // LANES):
        lanes = slice(LANES * cl, LANES * (cl + 1))
        taps = [cw_ref[k, :, lanes] for k in range(CONV_KERNEL)]

        def conv_chunk(c, _, lanes=lanes, taps=taps):
            r0 = pl.multiple_of(c * rc, rc)
            accs = [None] * (rc // NBATCH)
            for t in range(rc // NBATCH + CONV_KERNEL - 1):
                h = hc_ref[pl.ds(r0 + NBATCH * t, NBATCH), lanes]
                for rr in range(rc // NBATCH):
                    k = t - rr
                    if 0 <= k < CONV_KERNEL:
                        accs[rr] = h * taps[k] if accs[rr] is None else accs[rr] + h * taps[k]
            gl_ref[pl.ds(r0, rc), lanes] = jnp.concatenate(accs, axis=0)
            return 0

        lax.fori_loop(0, rows // rc, conv_chunk, 0)
    hc_ref[0:CONV_HALO, :] = hc_ref[rows:rows + CONV_HALO, :]
    h = gl_ref[...] + cb_ref[...]
    mu = jnp.mean(h, axis=-1, keepdims=True)
    var = jnp.mean(jnp.square(h - mu), axis=-1, keepdims=True)
    hf = (h - mu) * lax.rsqrt(var + NORM_EPS) * lng_ref[...] + lnb_ref[...]
    z_b = proj_ref[:, o + 2 * CONV_WIDTH:o + 3 * CONV_WIDTH]
    ycat_ref[:, S5_WIDTH:S5_WIDTH + CONV_WIDTH] = (_silu(hf) * _silu(z_b)).astype(BF16)

    y = _dot(ycat_ref[...], wout_ref[...])
    for t in range(tt):
        h = y[NBATCH * t:NBATCH * (t + 1), :] + x_ref[:, t * D_MODEL:(t + 1) * D_MODEL]
        for c in range(D_MODEL // LANES):
            tm_ref[c, NBATCH * t:NBATCH * (t + 1), :] = h[:, LANES * c:LANES * (c + 1)]
    for b in range(NBATCH):
        for c in range(D_MODEL // LANES):
            o_ref[b, :, LANES * c:LANES * (c + 1)] = tm_ref[c, pl.ds(b, tt, stride=NBATCH), :]


def _layer0(x2, gain, w_in, a_re, a_im, bc, cc, d_skip, w_glu, b_glu, conv_w, conv_b, ln_g, ln_b, w_out, *, seq, tt):
    rows = NBATCH * tt
    const = lambda shape: pl.BlockSpec(shape, lambda i: (0,) * len(shape))
    return pl.pallas_call(
        functools.partial(_layer0_kernel, tt=tt),
        grid=(seq // tt,),
        in_specs=[
            pl.BlockSpec((NBATCH, tt * D_MODEL), lambda i: (0, i)),
            const((1, D_MODEL)),
            const((D_MODEL, EVEN_IN)),
            const((NBATCH, S5_NSTATE)),
            const((NBATCH, S5_NSTATE)),
            const((S5_SLABS, LANES, 2 * S5_NSTATE // S5_SLABS)),
            const((S5_SLABS, 2 * S5_NSTATE // S5_SLABS, LANES)),
            const((1, S5_WIDTH)),
            const((S5_WIDTH, S5_WIDTH)),
            const((1, S5_WIDTH)),
            const((CONV_KERNEL, NBATCH, CONV_WIDTH)),
            const((1, CONV_WIDTH)),
            const((1, CONV_WIDTH)),
            const((1, CONV_WIDTH)),
            const((S5_WIDTH + CONV_WIDTH, D_MODEL)),
        ],
        out_specs=pl.BlockSpec((NBATCH, tt, D_MODEL), lambda i: (0, i, 0)),
        out_shape=jax.ShapeDtypeStruct((NBATCH, seq, D_MODEL), F32),
        scratch_shapes=[
            pltpu.VMEM((rows, D_MODEL), BF16),
            pltpu.VMEM((rows, EVEN_IN), F32),
            pltpu.VMEM((rows, 2 * S5_NSTATE), F32),
            pltpu.VMEM((NBATCH, 2 * S5_NSTATE), F32),
            pltpu.VMEM((CONV_HALO + rows, CONV_WIDTH), F32),
            pltpu.VMEM((rows, S5_WIDTH), F32),
            pltpu.VMEM((rows, S5_WIDTH + CONV_WIDTH), BF16),
            pltpu.VMEM((D_MODEL // LANES, rows, LANES), F32),
        ],
        compiler_params=pltpu.CompilerParams(dimension_semantics=("arbitrary",), vmem_limit_bytes=VMEM_LIMIT),
        name="layer0",
    )(x2, gain, w_in, a_re, a_im, bc, cc, d_skip, w_glu, b_glu, conv_w, conv_b, ln_g, ln_b, w_out)


def _inproj1_kernel(h_ref, g_ref, w_ref, cos_ref, sin_ref,
                    q_ref, k_ref, qi_ref, ki_ref, wi_ref, gate_ref, v_ref):
    h = h_ref[...]
    ms = jnp.mean(h * h, axis=-1, keepdims=True)
    hn = (h * lax.rsqrt(ms + NORM_EPS) * g_ref[...]).astype(BF16)
    proj = _dot(hn, w_ref[...])
    cos_t = cos_ref[...]
    sin_t = sin_ref[...]
    lane = lax.broadcasted_iota(I32, cos_t.shape, 1)
    first = (lane & (HEAD_DIM - 1)) < (ROT_DIM // 2)
    low = lane < HEAD_DIM

    def slab(c):
        return proj[:, LANES * c:LANES * (c + 1)]

    def rope(xs, ct, st):
        partner = jnp.where(first, pltpu.roll(xs, LANES - ROT_DIM // 2, 1), pltpu.roll(xs, ROT_DIM // 2, 1))
        return xs * ct + partner * st

    q_scale = (HEAD_DIM ** -0.5) * LOG2E
    for c in range(8):
        q_ref[:, LANES * c:LANES * (c + 1)] = (rope(slab(c), cos_t, sin_t) * q_scale).astype(BF16)
    for c in range(2):
        k_ref[:, LANES * c:LANES * (c + 1)] = rope(slab(8 + c), cos_t, sin_t).astype(BF16)
    for c in range(4):
        qi_ref[:, LANES * c:LANES * (c + 1)] = rope(slab(10 + c), cos_t, sin_t).astype(BF16)
    kiw = rope(slab(14), jnp.where(low, cos_t, 1.0), jnp.where(low, sin_t, 0.0))
    ki_ref[...] = kiw.astype(BF16)
    wi_ref[...] = kiw
    for c in range(8):
        gate_ref[:, LANES * c:LANES * (c + 1)] = _silu(slab(15 + c)).astype(BF16)
    for c in range(2):
        v_ref[:, LANES * c:LANES * (c + 1)] = slab(23 + c).astype(BF16)


def _inproj1(h1, gain, w, cos_t, sin_t, *, seq, rows):
    n = seq * NBATCH
    const = lambda shape: pl.BlockSpec(shape, lambda i: (0,) * len(shape))
    rb = lambda w_: pl.BlockSpec((rows, w_), lambda i: (i, 0))
    pos = pl.BlockSpec((rows, LANES), lambda i: (i % (seq // rows), 0))
    return pl.pallas_call(
        _inproj1_kernel,
        grid=(n // rows,),
        in_specs=[rb(D_MODEL), const((1, D_MODEL)), const((D_MODEL, ODD_PAD)), pos, pos],
        out_specs=[rb(ATT_WIDTH), rb(KV_WIDTH), rb(IDX_HEADS * IDX_DIM), rb(LANES), rb(LANES), rb(ATT_WIDTH),
                   rb(KV_WIDTH)],
        out_shape=[
            jax.ShapeDtypeStruct((n, ATT_WIDTH), BF16),
            jax.ShapeDtypeStruct((n, KV_WIDTH), BF16),
            jax.ShapeDtypeStruct((n, IDX_HEADS * IDX_DIM), BF16),
            jax.ShapeDtypeStruct((n, LANES), BF16),
            jax.ShapeDtypeStruct((n, LANES), F32),
            jax.ShapeDtypeStruct((n, ATT_WIDTH), BF16),
            jax.ShapeDtypeStruct((n, KV_WIDTH), BF16),
        ],
        compiler_params=pltpu.CompilerParams(dimension_semantics=("arbitrary",), vmem_limit_bytes=VMEM_LIMIT),
        name="inproj1",
    )(h1, gain, w, cos_t, sin_t)


def _attn_kernel(q_ref, qi_ref, wi_ref, gate_ref, h1_ref, k_ref, v_ref, ki_ref, wout_ref, fg_ref, o_ref,
                 vt_ref, sc_ref, qh_ref, qih_ref, acc_ref, m_ref, jcut_ref, s_ref, mb_ref, thr_ref, cnt_ref,
                 *, seq, ksel):
    tq = ATT_TILE
    i = pl.program_id(1)
    nblk = i + 1

    @pl.when(i == 0)
    def _():
        def transpose_v(kb, _):
            r0 = pl.multiple_of(kb * tq, tq)
            vt = v_ref[pl.ds(r0, tq), :].astype(F32).T
            for j in range(N_KV_HEADS):
                vt_ref[kb, j, 0:HEAD_DIM, :] = vt[HEAD_DIM * j:HEAD_DIM * (j + 1), :].astype(BF16)
                vt_ref[kb, j, HEAD_DIM:PV_ROWS, :] = jnp.ones((PV_ROWS - HEAD_DIM, tq), BF16)
            return 0
        lax.fori_loop(0, seq // tq, transpose_v, 0)

    zero_half = jnp.zeros((HEAD_DIM, tq), BF16)
    for c in range(ATT_WIDTH // LANES):
        t = q_ref[:, LANES * c:LANES * (c + 1)].astype(F32).T
        for half in range(2):
            j, g = divmod(2 * c + half, Q_PER_KV)
            lo = (j % 2) * HEAD_DIM
            cols = slice(tq * g, tq * (g + 1))
            qh_ref[j, lo:lo + HEAD_DIM, cols] = t[HEAD_DIM * half:HEAD_DIM * (half + 1), :].astype(BF16)
            qh_ref[j, HEAD_DIM - lo:2 * HEAD_DIM - lo, cols] = zero_half
    for c in range(IDX_HEADS * IDX_DIM // LANES):
        t = qi_ref[:, LANES * c:LANES * (c + 1)].astype(F32).T
        for half in range(2):
            h = 2 * c + half
            qih_ref[0:IDX_DIM, tq * h:tq * (h + 1)] = t[IDX_DIM * half:IDX_DIM * (half + 1), :].astype(BF16)
    qih_ref[IDX_DIM:LANES, :] = jnp.zeros((LANES - IDX_DIM, IDX_HEADS * tq), BF16)
    w8 = wi_ref[...].T[IDX_DIM:IDX_DIM + IDX_HEADS, :] * (IDX_HEADS ** -0.5) * (IDX_DIM ** -0.5)

    row = lax.broadcasted_iota(I32, (tq, tq), 0)
    col = lax.broadcasted_iota(I32, (tq, tq), 1)
    row_chunk = row >> 6
    col_chunk = col >> 6

    def tree(op, x):
        parts = [x[8 * r:8 * (r + 1)] for r in range(x.shape[0] // 8)]
        while len(parts) > 1:
            parts = [op(parts[a], parts[a + 1]) for a in range(0, len(parts), 2)]
        return parts[0]

    def score_block(kb, carry):
        smax, smin = carry
        r0 = pl.multiple_of(kb * tq, tq)
        s = _dot(ki_ref[pl.ds(r0, tq), :], qih_ref[...])
        acc = jnp.maximum(s[:, 0:tq], 0.0) * w8[0:1, :]
        for h in range(1, IDX_HEADS):
            acc = acc + jnp.maximum(s[:, tq * h:tq * (h + 1)], 0.0) * w8[h:h + 1, :]
        acc = jnp.where(acc == 0.0, 0.0, acc)
        later = ((kb - i) * (tq // CHUNK) + row_chunk) > col_chunk
        sc_ref[pl.ds(r0, tq), :] = jnp.where(later, -jnp.inf, acc)
        smax = jnp.maximum(smax, tree(jnp.maximum, jnp.where(later, -jnp.inf, acc)))
        smin = jnp.minimum(smin, tree(jnp.minimum, jnp.where(later, jnp.inf, acc)))
        return smax, smin

    smax, smin = lax.fori_loop(0, nblk, score_block,
                               (jnp.full((8, tq), -jnp.inf, F32), jnp.full((8, tq), jnp.inf, F32)))
    smax = jnp.max(smax, axis=0, keepdims=True)
    smin = jnp.min(smin, axis=0, keepdims=True)

    def lane_pass(elem, op, init):
        def body(kb, acc):
            r0 = pl.multiple_of(kb * tq, tq)
            return op(acc, tree(op, elem(sc_ref[pl.ds(r0, tq), :], r0)))
        return lax.fori_loop(0, nblk, body, init)

    def count(pred):
        acc = lane_pass(lambda blk, r0: jnp.where(pred(blk, r0), 1, 0).astype(I32), jnp.add,
                        jnp.zeros((8, tq), I32))
        return jnp.sum(acc, axis=0, keepdims=True)

    def max_below(bound):
        acc = lane_pass(lambda blk, r0: jnp.where(blk < bound, blk, -jnp.inf), jnp.maximum,
                        jnp.full((8, tq), -jnp.inf, F32))
        return jnp.max(acc, axis=0, keepdims=True)

    def as_threshold(key):
        ks = key ^ INT_MIN
        return pltpu.bitcast(jnp.where(ks >= 0, ks, ks ^ np.int32(0x7FFFFFFF)), F32)

    def next_up(x):
        bits = pltpu.bitcast(x, I32)
        ks = jnp.where(bits >= 0, bits, bits ^ np.int32(0x7FFFFFFF)) + 1
        return pltpu.bitcast(jnp.where(ks >= 0, ks, ks ^ np.int32(0x7FFFFFFF)), F32)

    n_adm = (CHUNK * (tq // CHUNK) * i + CHUNK) + CHUNK * col_chunk[0:1, :]
    few = n_adm <= ksel

    def halve(_, st):
        lo, hi, c_lo, c_hi = st
        t = 0.5 * lo + 0.5 * hi
        inside = (t > lo) & (t < hi)
        t = jnp.where(inside, t, lo)
        c = count(lambda blk, r0: blk >= t)
        up = inside & (c >= ksel)
        dn = inside & (c < ksel)
        return jnp.where(up, t, lo), jnp.where(dn, t, hi), jnp.where(up, c, c_lo), jnp.where(dn, c, c_hi)

    zero = jnp.zeros((1, tq), I32)
    lo, hi, c_lo, c_hi = lax.fori_loop(
        0, BISECT_STEPS, halve, (smin, next_up(jnp.minimum(smax, F32_MAX)), n_adm, zero))

    def unresolved(c_lo, exact):
        return (c_lo > ksel) & (exact == 0) & jnp.logical_not(few)

    def peel_cond(st):
        it, lo, hi, c_lo, c_hi, exact = st
        return (it < PEEL_STEPS) & (jnp.max(jnp.where(unresolved(c_lo, exact), 1, 0)) > 0)

    def peel(st):
        it, lo, hi, c_lo, c_hi, exact = st
        v = max_below(hi)
        c_v = count(lambda blk, r0: blk >= v)
        act = unresolved(c_lo, exact)
        hit = act & (c_v >= ksel)
        miss = act & (c_v < ksel)
        return (it + 1, jnp.where(hit, v, lo), jnp.where(miss, v, hi), jnp.where(hit, c_v, c_lo),
                jnp.where(miss, c_v, c_hi), jnp.where(hit, 1, exact))

    _, lo, hi, c_lo, c_hi, exact = lax.while_loop(peel_cond, peel, (jnp.int32(0), lo, hi, c_lo, c_hi, zero))
    thr_ref[...] = jnp.where(few, -F32_MAX, lo)
    cnt_ref[0:1, :] = c_lo
    cnt_ref[1:2, :] = c_hi

    @pl.when(jnp.max(jnp.where(unresolved(c_lo, exact), 1, 0)) > 0)
    def _():
        def bisect(it, carry):
            t, cnt_t = carry
            cand = t | (jnp.int32(1) << (31 - it))
            cand_f = as_threshold(cand)
            c = count(lambda blk, r0: blk >= cand_f)
            ok = c >= ksel
            return jnp.where(ok, cand, t), jnp.where(ok, c, cnt_t)

        thr_key, cnt_ge = lax.fori_loop(0, 32, bisect, (zero, zero))
        t = as_threshold(thr_key)
        t = jnp.maximum(jnp.where(t != t, -F32_MAX, t), -F32_MAX)
        thr_ref[...] = t
        cnt_ref[0:1, :] = cnt_ge
        cnt_ref[1:2, :] = count(lambda blk, r0: blk > t)

    thr = thr_ref[...]
    cnt_ge = cnt_ref[0:1, :]
    need = ksel - cnt_ref[1:2, :]

    jbits = int(seq).bit_length()
    jcut_ref[...] = jnp.full((1, tq), (1 << jbits) - 1, I32)

    @pl.when(jnp.max(jnp.where(cnt_ge > ksel, 1, 0)) > 0)
    def _():
        def jbisect(it, jc):
            cand = jc | (jnp.int32(1) << (jbits - 1 - it))
            f = count(lambda blk, r0: (blk == thr) & ((r0 + row) < cand))
            return jnp.where(f <= need, cand, jc)
        jcut_ref[...] = lax.fori_loop(0, jbits, jbisect, jnp.zeros((1, tq), I32))

    jcut = jcut_ref[...]

    m_ref[...] = jnp.full(m_ref.shape, NEG_BIG, F32)
    acc_ref[...] = jnp.zeros(acc_ref.shape, F32)

    def logits(kb, slot):
        r0 = pl.multiple_of(kb * tq, tq)
        blk = sc_ref[pl.ds(r0, tq), :]
        sel = (blk > thr) | ((blk == thr) & ((r0 + row) < jcut))
        bias = jnp.where(sel, 0.0, -jnp.inf)
        bias = jnp.concatenate([bias] * Q_PER_KV, axis=1)
        for j in range(N_KV_HEADS):
            s = _dot(k_ref[pl.ds(r0, tq), LANES * (j // 2):LANES * (j // 2 + 1)], qh_ref[j]) + bias
            s_ref[slot, j] = s
            mb_ref[slot, j:j + 1, :] = jnp.max(s, axis=0, keepdims=True)

    def accumulate(kb, slot):
        for j in range(N_KV_HEADS):
            m_old = m_ref[j:j + 1, :]
            m_new = jnp.maximum(m_old, mb_ref[slot, j:j + 1, :])
            m_ref[j:j + 1, :] = m_new
            p = jnp.exp2(s_ref[slot, j] - m_new).astype(BF16)
            acc_ref[j] = acc_ref[j] * jnp.exp2(m_old - m_new) + _dot(vt_ref[kb, j], p)

    def attend_pair(t, _):
        kb = 2 * t
        logits(kb + 1, 1)
        accumulate(kb, 0)
        logits(kb + 2, 0)
        accumulate(kb + 1, 1)
        return 0

    logits(0, 0)
    npair = (nblk - 1) // 2
    lax.fori_loop(0, npair, attend_pair, 0)
    done = 2 * npair

    @pl.when(nblk - done == 1)
    def _():
        accumulate(done, 0)

    @pl.when(nblk - done == 2)
    def _():
        logits(done + 1, 1)
        accumulate(done, 0)
        accumulate(done + 1, 1)

    slabs = []
    for pair in range(N_HEADS // 2):
        j, g0 = pair // 2, 2 * (pair % 2)
        inv_l = 1.0 / acc_ref[j, HEAD_DIM:HEAD_DIM + 1, tq * g0:tq * (g0 + 2)]
        o2 = acc_ref[j, 0:HEAD_DIM, tq * g0:tq * (g0 + 2)] * inv_l
        slabs.append(jnp.concatenate([o2[:, 0:tq], o2[:, tq:2 * tq]], axis=0).T)
    att = jnp.concatenate(slabs, axis=1)
    y = _dot((att * gate_ref[...].astype(F32)).astype(BF16), wout_ref[...])
    hh = h1_ref[...] + y
    ms = jnp.mean(hh * hh, axis=-1, keepdims=True)
    o_ref[0] = hh * lax.rsqrt(ms + NORM_EPS) * fg_ref[...]


def _attention(q, qi, wi, gate, h1, k, v, ki, w_out, final_g, *, seq, ksel):
    tq = ATT_TILE
    ntile = seq // tq
    tile = lambda w_: pl.BlockSpec((tq, w_), lambda b, i: (b * ntile + i, 0))
    whole = lambda w_: pl.BlockSpec((seq, w_), lambda b, i: (b, 0))
    const = lambda shape: pl.BlockSpec(shape, lambda b, i: (0,) * len(shape))
    return pl.pallas_call(
        functools.partial(_attn_kernel, seq=seq, ksel=ksel),
        grid=(NBATCH, seq // tq),
        in_specs=[tile(ATT_WIDTH), tile(IDX_HEADS * IDX_DIM), tile(LANES), tile(ATT_WIDTH), tile(D_MODEL),
                  whole(KV_WIDTH), whole(KV_WIDTH), whole(LANES),
                  const((ATT_WIDTH, D_MODEL)), const((1, D_MODEL))],
        out_specs=pl.BlockSpec((1, tq, D_MODEL), lambda b, i: (b, i, 0)),
        out_shape=jax.ShapeDtypeStruct((NBATCH, seq, D_MODEL), F32),
        scratch_shapes=[
            pltpu.VMEM((seq // tq, N_KV_HEADS, PV_ROWS, tq), BF16),
            pltpu.VMEM((seq, tq), F32),
            pltpu.VMEM((N_KV_HEADS, LANES, Q_PER_KV * tq), BF16),
            pltpu.VMEM((LANES, IDX_HEADS * tq), BF16),
            pltpu.VMEM((N_KV_HEADS, PV_ROWS, Q_PER_KV * tq), F32),
            pltpu.VMEM((N_KV_HEADS, Q_PER_KV * tq), F32),
            pltpu.VMEM((1, tq), I32),
            pltpu.VMEM((2, N_KV_HEADS, tq, Q_PER_KV * tq), F32),
            pltpu.VMEM((2, N_KV_HEADS, Q_PER_KV * tq), F32),
            pltpu.VMEM((1, tq), F32),
            pltpu.VMEM((2, tq), I32),
        ],
        compiler_params=pltpu.CompilerParams(dimension_semantics=("arbitrary", "arbitrary"),
                                             vmem_limit_bytes=VMEM_LIMIT),
        name="dsa_attention",
    )(q, qi, wi, gate, h1, k, v, ki, w_out, final_g)


def _rope_tables(seq):
    pos = jnp.arange(seq, dtype=F32)
    inv = ROPE_THETA ** (-jnp.arange(0, ROT_DIM, 2, dtype=F32) / ROT_DIM)
    ang = pos[:, None] * inv[None, :]
    cos, sin = jnp.cos(ang), jnp.sin(ang)
    ones = jnp.ones((seq, HEAD_DIM - ROT_DIM), F32)
    return (jnp.concatenate([cos, cos, ones, cos, cos, ones], axis=1),
            jnp.concatenate([-sin, sin, 0.0 * ones, -sin, sin, 0.0 * ones], axis=1))


def _reorder_odd_weights(w):
    o = np.cumsum([0, ATT_WIDTH, ATT_WIDTH, KV_WIDTH, KV_WIDTH, IDX_HEADS * IDX_DIM, IDX_DIM, IDX_HEADS]).tolist()
    q, z, k, v, qi, ki, wi = [w[:, o[j]:o[j + 1]] for j in range(7)]
    pad = jnp.zeros((w.shape[0], LANES - IDX_DIM - IDX_HEADS), w.dtype)
    return jnp.concatenate([q, k, qi, ki, wi, pad, z, v], axis=1)


def kernel(x, norm_g, e_w_in, e_lam_re, e_lam_im, e_log_step, e_b_re, e_b_im, e_c_re, e_c_im, e_d_skip,
           e_w_glu, e_b_glu, e_conv_w, e_conv_b, e_ln_g, e_ln_b, e_w_out, o_w_in, o_w_out, final_g):
    bsz, seq, dm = x.shape
    assert bsz == NBATCH and dm == D_MODEL and seq % ATT_TILE == 0
    assert norm_g.shape[0] == 2 and e_w_in.shape[0] == 1 and o_w_in.shape[0] == 1
    tt = 64
    ksel = min(TOPK_MAX, seq // 4)
    row = lambda a: a.reshape(1, -1).astype(F32)

    ab_re, ab_im, bb_re, bb_im = _s5_prep(e_lam_re[0], e_lam_im[0], e_log_step[0], e_b_re[0], e_b_im[0])
    bc, cc = _s5_block_weights(bb_re, bb_im, e_c_re[0], e_c_im[0])
    a_re = jnp.broadcast_to(ab_re.reshape(1, S5_NSTATE), (NBATCH, S5_NSTATE))
    a_im = jnp.broadcast_to(ab_im.reshape(1, S5_NSTATE), (NBATCH, S5_NSTATE))
    conv_w = jnp.broadcast_to(e_conv_w[0][:, None, :], (CONV_KERNEL, NBATCH, CONV_WIDTH))

    h1 = _layer0(x.reshape(NBATCH, seq * D_MODEL), row(norm_g[0]), e_w_in[0].astype(BF16), a_re, a_im, bc, cc,
                 row(e_d_skip[0]), e_w_glu[0].astype(BF16), row(e_b_glu[0]), conv_w, row(e_conv_b[0]),
                 row(e_ln_g[0]), row(e_ln_b[0]), e_w_out[0].astype(BF16), seq=seq, tt=tt)
    h1 = h1.reshape(NBATCH * seq, D_MODEL)

    cos_t, sin_t = _rope_tables(seq)
    q, k, qi, ki, wi, gate, v = _inproj1(h1, row(norm_g[1]), _reorder_odd_weights(o_w_in[0]).astype(BF16),
                                         cos_t, sin_t, seq=seq, rows=512)
    return _attention(q, qi, wi, gate, h1, k, v, ki, o_w_out[0].astype(BF16), row(final_g), seq=seq, ksel=ksel)
```

```python
import functools
import math

import numpy as np
import jax
import jax.numpy as jnp
from jax import lax
from jax.experimental import pallas as pl
from jax.experimental.pallas import tpu as pltpu

F32 = jnp.float32
BF16 = jnp.bfloat16
I32 = jnp.int32

NBATCH = 8
D_MODEL = 1024
CHUNK = 64
NORM_EPS = 1e-6
S5_WIDTH = 512
S5_GROUP = 16
S5_GROUPS = 32
S5_STATE = 64
S5_NSTATE = S5_GROUPS * S5_STATE
S5_SLABS = 4
CONV_WIDTH = 512
CONV_KERNEL = 31
CONV_HALO = (CONV_KERNEL - 1) * NBATCH
EVEN_IN = 2 * S5_WIDTH + 3 * CONV_WIDTH
N_HEADS = 16
HEAD_DIM = 64
N_KV_HEADS = 4
Q_PER_KV = 4
ATT_WIDTH = 1024
KV_WIDTH = 256
IDX_HEADS = 8
IDX_DIM = 64
TOPK_MAX = 256
ROPE_THETA = 500000.0
ROT_DIM = 16
LANES = 128
ODD_PAD = 3200
ATT_TILE = 256
PV_ROWS = HEAD_DIM + 16
BISECT_STEPS = 16
PEEL_STEPS = 6
SHIFT_MARGIN = 1.02
SHIFT_MAX = 50.0
INT_MIN = np.int32(-2 ** 31)
F32_MAX = float(np.finfo(np.float32).max)
NEG_BIG = -1e38
LOG2E = math.log2(math.e)

VMEM_LIMIT = 56 * 1024 * 1024


def _dot(a, b):
    return jnp.dot(a, b, preferred_element_type=F32)


def _dot_nt(a, b):
    return lax.dot_general(a, b, (((1,), (1,)), ((), ())), preferred_element_type=F32)


def _sigmoid(x):
    return 1.0 / (1.0 + jnp.exp(-x))


def _silu(x):
    return x * _sigmoid(x)


def _gelu_tanh(x):
    c = math.sqrt(2.0 / math.pi)
    return 0.5 * x * (1.0 + jnp.tanh(c * (x + 0.044715 * (x * x * x))))


def _s5_prep_kernel(lr_ref, li_ref, ls_ref, br_ref, bi_ref, abr_ref, abi_ref, bbr_ref, bbi_ref):
    lr = jnp.minimum(lr_ref[...], -1e-4)
    li = li_ref[...]
    dt = jnp.exp(ls_ref[...])
    mag = jnp.exp(lr * dt)
    ab_re = mag * jnp.cos(li * dt)
    ab_im = mag * jnp.sin(li * dt)
    den = lr * lr + li * li
    nr = ab_re - 1.0
    ni = ab_im
    k_re = (nr * lr + ni * li) / den
    k_im = (ni * lr - nr * li) / den
    br = br_ref[...]
    bi = bi_ref[...]
    abr_ref[...] = ab_re
    abi_ref[...] = ab_im
    bbr_ref[...] = k_re * br - k_im * bi
    bbi_ref[...] = k_re * bi + k_im * br


def _s5_prep(lam_re, lam_im, log_step, b_re, b_im):
    g, n, c = b_re.shape
    shape2 = (g * n * c // LANES, LANES)
    expand = lambda a: jnp.broadcast_to(a[:, :, None], (g, n, c)).reshape(shape2)
    ls = jnp.broadcast_to(log_step[:, None], (g, n))
    outs = pl.pallas_call(
        _s5_prep_kernel,
        out_shape=[jax.ShapeDtypeStruct(shape2, F32)] * 4,
        name="s5_prep",
    )(expand(lam_re), expand(lam_im), expand(ls), b_re.reshape(shape2), b_im.reshape(shape2))
    ab_re, ab_im, bb_re, bb_im = [o.reshape(g, n, c) for o in outs]
    return ab_re[:, :, 0], ab_im[:, :, 0], bb_re, bb_im


def _s5_block_weights(bb_re, bb_im, c_re, c_im):
    gl = S5_GROUPS // S5_SLABS
    eye = jnp.eye(gl, dtype=F32)

    def in_map(bb):
        t = bb.reshape(S5_SLABS, gl, S5_STATE, S5_GROUP).transpose(0, 1, 3, 2)
        t = t[:, :, :, None, :] * eye[None, :, None, :, None]
        return t.reshape(S5_SLABS, gl * S5_GROUP, gl * S5_STATE)

    def out_map(cc):
        t = cc.reshape(S5_SLABS, gl, S5_GROUP, S5_STATE).transpose(0, 1, 3, 2)
        t = t[:, :, :, None, :] * eye[None, :, None, :, None]
        return t.reshape(S5_SLABS, gl * S5_STATE, gl * S5_GROUP)

    bc = jnp.concatenate([in_map(bb_re), in_map(bb_im)], axis=2).astype(BF16)
    cc = jnp.concatenate([out_map(c_re), out_map(-c_im)], axis=1).astype(BF16)
    return bc, cc


def _layer0_kernel(x_ref, g_ref, win_ref, are_ref, aim_ref, bc_ref, cc_ref, dsk_ref, wglu_ref, bglu_ref,
                   cw_ref, cb_ref, lng_ref, lnb_ref, wout_ref, o_ref,
                   hn_ref, proj_ref, xs_ref, st_ref, hc_ref, gl_ref, ycat_ref, tm_ref, *, tt):
    rows = NBATCH * tt
    half = S5_NSTATE
    slab_w = half // S5_SLABS

    @pl.when(pl.program_id(0) == 0)
    def _():
        st_ref[...] = jnp.zeros_like(st_ref)
        hc_ref[0:CONV_HALO, :] = jnp.zeros((CONV_HALO, CONV_WIDTH), F32)

    gain = g_ref[...]
    for tp in range(tt // 2):
        xa = x_ref[:, (2 * tp) * D_MODEL:(2 * tp + 1) * D_MODEL]
        xb = x_ref[:, (2 * tp + 1) * D_MODEL:(2 * tp + 2) * D_MODEL]
        xx = jnp.concatenate([xa, xb], axis=0)
        ms = jnp.mean(xx * xx, axis=-1, keepdims=True)
        hn_ref[16 * tp:16 * tp + 16, :] = (xx * lax.rsqrt(ms + NORM_EPS) * gain).astype(BF16)

    proj_ref[...] = _dot(hn_ref[...], win_ref[...])

    for m in range(S5_SLABS):
        u_m = proj_ref[:, LANES * m:LANES * (m + 1)].astype(BF16)
        xm = _dot(u_m, bc_ref[m])
        xs_ref[:, slab_w * m:slab_w * (m + 1)] = xm[:, :slab_w]
        xs_ref[:, half + slab_w * m:half + slab_w * (m + 1)] = xm[:, slab_w:]

    cw = 512
    for cc_i in range(half // cw):
        lo = cc_i * cw
        a_re = are_ref[:, lo:lo + cw]
        a_im = aim_ref[:, lo:lo + cw]

        def step(t, carry, lo=lo, a_re=a_re, a_im=a_im):
            s_re, s_im = carry
            r0 = pl.multiple_of(t * NBATCH, NBATCH)
            x_re = xs_ref[pl.ds(r0, NBATCH), lo:lo + cw]
            x_im = xs_ref[pl.ds(r0, NBATCH), half + lo:half + lo + cw]
            n_re = a_re * s_re - a_im * s_im + x_re
            n_im = a_re * s_im + a_im * s_re + x_im
            xs_ref[pl.ds(r0, NBATCH), lo:lo + cw] = n_re
            xs_ref[pl.ds(r0, NBATCH), half + lo:half + lo + cw] = n_im
            return n_re, n_im

        s_re, s_im = lax.fori_loop(0, tt, step, (st_ref[:, lo:lo + cw], st_ref[:, half + lo:half + lo + cw]),
                                   unroll=True)
        st_ref[:, lo:lo + cw] = s_re
        st_ref[:, half + lo:half + lo + cw] = s_im

    for m in range(S5_SLABS):
        s_re = xs_ref[:, slab_w * m:slab_w * (m + 1)].astype(BF16)
        s_im = xs_ref[:, half + slab_w * m:half + slab_w * (m + 1)].astype(BF16)
        y_m = _dot(s_re, cc_ref[m, 0:slab_w, :]) + _dot(s_im, cc_ref[m, slab_w:2 * slab_w, :])
        y_m = y_m + proj_ref[:, LANES * m:LANES * (m + 1)] * dsk_ref[:, LANES * m:LANES * (m + 1)]
        gl_ref[:, LANES * m:LANES * (m + 1)] = _gelu_tanh(y_m)
    gl = gl_ref[...]
    out_a = gl * _sigmoid(_dot(gl.astype(BF16), wglu_ref[...]) + bglu_ref[...])
    ycat_ref[:, 0:S5_WIDTH] = (out_a * _silu(proj_ref[:, S5_WIDTH:2 * S5_WIDTH])).astype(BF16)

    o = 2 * S5_WIDTH
    hc_ref[CONV_HALO:CONV_HALO + rows, :] = (
        proj_ref[:, o:o + CONV_WIDTH] * _sigmoid(proj_ref[:, o + CONV_WIDTH:o + 2 * CONV_WIDTH]))
    rc = 64
    for cl in range(CONV_WIDTH // LANES):
        lanes = slice(LANES * cl, LANES * (cl + 1))
        taps = [cw_ref[k, :, lanes] for k in range(CONV_KERNEL)]

        def conv_chunk(c, _, lanes=lanes, taps=taps):
            r0 = pl.multiple_of(c * rc, rc)
            accs = [None] * (rc // NBATCH)
            for t in range(rc // NBATCH + CONV_KERNEL - 1):
                h = hc_ref[pl.ds(r0 + NBATCH * t, NBATCH), lanes]
                for rr in range(rc // NBATCH):
                    k = t - rr
                    if 0 <= k < CONV_KERNEL:
                        accs[rr] = h * taps[k] if accs[rr] is None else accs[rr] + h * taps[k]
            gl_ref[pl.ds(r0, rc), lanes] = jnp.concatenate(accs, axis=0)
            return 0

        lax.fori_loop(0, rows // rc, conv_chunk, 0)
    hc_ref[0:CONV_HALO, :] = hc_ref[rows:rows + CONV_HALO, :]
    h = gl_ref[...] + cb_ref[...]
    mu = jnp.mean(h, axis=-1, keepdims=True)
    var = jnp.mean(jnp.square(h - mu), axis=-1, keepdims=True)
    hf = (h - mu) * lax.rsqrt(var + NORM_EPS) * lng_ref[...] + lnb_ref[...]
    z_b = proj_ref[:, o + 2 * CONV_WIDTH:o + 3 * CONV_WIDTH]
    ycat_ref[:, S5_WIDTH:S5_WIDTH + CONV_WIDTH] = (_silu(hf) * _silu(z_b)).astype(BF16)

    y = _dot(ycat_ref[...], wout_ref[...])
    for t in range(tt):
        h = y[NBATCH * t:NBATCH * (t + 1), :] + x_ref[:, t * D_MODEL:(t + 1) * D_MODEL]
        for c in range(D_MODEL // LANES):
            tm_ref[c, NBATCH * t:NBATCH * (t + 1), :] = h[:, LANES * c:LANES * (c + 1)]
    for b in range(NBATCH):
        for c in range(D_MODEL // LANES):
            o_ref[b, :, LANES * c:LANES * (c + 1)] = tm_ref[c, pl.ds(b, tt, stride=NBATCH), :]


def _layer0(x2, gain, w_in, a_re, a_im, bc, cc, d_skip, w_glu, b_glu, conv_w, conv_b, ln_g, ln_b, w_out, *, seq, tt):
    rows = NBATCH * tt
    const = lambda shape: pl.BlockSpec(shape, lambda i: (0,) * len(shape))
    return pl.pallas_call(
        functools.partial(_layer0_kernel, tt=tt),
        grid=(seq // tt,),
        in_specs=[
            pl.BlockSpec((NBATCH, tt * D_MODEL), lambda i: (0, i)),
            const((1, D_MODEL)),
            const((D_MODEL, EVEN_IN)),
            const((NBATCH, S5_NSTATE)),
            const((NBATCH, S5_NSTATE)),
            const((S5_SLABS, LANES, 2 * S5_NSTATE // S5_SLABS)),
            const((S5_SLABS, 2 * S5_NSTATE // S5_SLABS, LANES)),
            const((1, S5_WIDTH)),
            const((S5_WIDTH, S5_WIDTH)),
            const((1, S5_WIDTH)),
            const((CONV_KERNEL, NBATCH, CONV_WIDTH)),
            const((1, CONV_WIDTH)),
            const((1, CONV_WIDTH)),
            const((1, CONV_WIDTH)),
            const((S5_WIDTH + CONV_WIDTH, D_MODEL)),
        ],
        out_specs=pl.BlockSpec((NBATCH, tt, D_MODEL), lambda i: (0, i, 0)),
        out_shape=jax.ShapeDtypeStruct((NBATCH, seq, D_MODEL), F32),
        scratch_shapes=[
            pltpu.VMEM((rows, D_MODEL), BF16),
            pltpu.VMEM((rows, EVEN_IN), F32),
            pltpu.VMEM((rows, 2 * S5_NSTATE), F32),
            pltpu.VMEM((NBATCH, 2 * S5_NSTATE), F32),
            pltpu.VMEM((CONV_HALO + rows, CONV_WIDTH), F32),
            pltpu.VMEM((rows, S5_WIDTH), F32),
            pltpu.VMEM((rows, S5_WIDTH + CONV_WIDTH), BF16),
            pltpu.VMEM((D_MODEL // LANES, rows, LANES), F32),
        ],
        compiler_params=pltpu.CompilerParams(dimension_semantics=("arbitrary",), vmem_limit_bytes=VMEM_LIMIT),
        name="layer0",
    )(x2, gain, w_in, a_re, a_im, bc, cc, d_skip, w_glu, b_glu, conv_w, conv_b, ln_g, ln_b, w_out)


def _inproj1_kernel(h_ref, g_ref, w_ref, cos_ref, sin_ref,
                    q_ref, k_ref, qi_ref, ki_ref, wi_ref, gate_ref, v_ref):
    h = h_ref[...]
    ms = jnp.mean(h * h, axis=-1, keepdims=True)
    hn = (h * lax.rsqrt(ms + NORM_EPS) * g_ref[...]).astype(BF16)
    proj = _dot(hn, w_ref[...])
    cos_t = cos_ref[...]
    sin_t = sin_ref[...]
    lane = lax.broadcasted_iota(I32, cos_t.shape, 1)
    first = (lane & (HEAD_DIM - 1)) < (ROT_DIM // 2)
    low = lane < HEAD_DIM

    def slab(c):
        return proj[:, LANES * c:LANES * (c + 1)]

    def rope(xs, ct, st):
        partner = jnp.where(first, pltpu.roll(xs, LANES - ROT_DIM // 2, 1), pltpu.roll(xs, ROT_DIM // 2, 1))
        return xs * ct + partner * st

    q_scale = (HEAD_DIM ** -0.5) * LOG2E
    for c in range(8):
        q_ref[:, LANES * c:LANES * (c + 1)] = (rope(slab(c), cos_t, sin_t) * q_scale).astype(BF16)
    for c in range(2):
        k_ref[:, LANES * c:LANES * (c + 1)] = rope(slab(8 + c), cos_t, sin_t).astype(BF16)
    for c in range(4):
        qi_ref[:, LANES * c:LANES * (c + 1)] = rope(slab(10 + c), cos_t, sin_t).astype(BF16)
    kiw = rope(slab(14), jnp.where(low, cos_t, 1.0), jnp.where(low, sin_t, 0.0))
    ki_ref[...] = kiw.astype(BF16)
    wi_ref[...] = kiw
    for c in range(8):
        gate_ref[:, LANES * c:LANES * (c + 1)] = _silu(slab(15 + c)).astype(BF16)
    for c in range(2):
        v_ref[:, LANES * c:LANES * (c + 1)] = slab(23 + c).astype(BF16)


def _inproj1(h1, gain, w, cos_t, sin_t, *, seq, rows):
    n = seq * NBATCH
    const = lambda shape: pl.BlockSpec(shape, lambda i: (0,) * len(shape))
    rb = lambda w_: pl.BlockSpec((rows, w_), lambda i: (i, 0))
    pos = pl.BlockSpec((rows, LANES), lambda i: (i % (seq // rows), 0))
    return pl.pallas_call(
        _inproj1_kernel,
        grid=(n // rows,),
        in_specs=[rb(D_MODEL), const((1, D_MODEL)), const((D_MODEL, ODD_PAD)), pos, pos],
        out_specs=[rb(ATT_WIDTH), rb(KV_WIDTH), rb(IDX_HEADS * IDX_DIM), rb(LANES), rb(LANES), rb(ATT_WIDTH),
                   rb(KV_WIDTH)],
        out_shape=[
            jax.ShapeDtypeStruct((n, ATT_WIDTH), BF16),
            jax.ShapeDtypeStruct((n, KV_WIDTH), BF16),
            jax.ShapeDtypeStruct((n, IDX_HEADS * IDX_DIM), BF16),
            jax.ShapeDtypeStruct((n, LANES), BF16),
            jax.ShapeDtypeStruct((n, LANES), F32),
            jax.ShapeDtypeStruct((n, ATT_WIDTH), BF16),
            jax.ShapeDtypeStruct((n, KV_WIDTH), BF16),
        ],
        compiler_params=pltpu.CompilerParams(dimension_semantics=("arbitrary",), vmem_limit_bytes=VMEM_LIMIT),
        name="inproj1",
    )(h1, gain, w, cos_t, sin_t)


def _attn_kernel(q_ref, qi_ref, wi_ref, gate_ref, h1_ref, k_ref, v_ref, ki_ref, wout_ref, fg_ref, o_ref,
                 vt_ref, sc_ref, qh_ref, qih_ref, acc_ref, m_ref, jcut_ref, s_ref, mb_ref, thr_ref, cnt_ref,
                 kn_ref, p_ref, *, seq, ksel):
    tq = ATT_TILE
    i = pl.program_id(1)
    nblk = i + 1

    @pl.when(i == 0)
    def _():
        def transpose_v(kb, _):
            r0 = pl.multiple_of(kb * tq, tq)
            vt = v_ref[pl.ds(r0, tq), :].astype(F32).T
            for j in range(N_KV_HEADS):
                vt_ref[kb, j, 0:HEAD_DIM, :] = vt[HEAD_DIM * j:HEAD_DIM * (j + 1), :].astype(BF16)
                vt_ref[kb, j, HEAD_DIM:PV_ROWS, :] = jnp.ones((PV_ROWS - HEAD_DIM, tq), BF16)
            return 0
        lax.fori_loop(0, seq // tq, transpose_v, 0)

        def key_norm(kb, mx):
            r0 = pl.multiple_of(kb * tq, tq)
            kt = k_ref[pl.ds(r0, tq), :].astype(F32).T
            sq = kt * kt
            for j in range(N_KV_HEADS):
                mx = jnp.maximum(mx, jnp.sum(sq[HEAD_DIM * j:HEAD_DIM * (j + 1), :], axis=0, keepdims=True))
            return mx
        kmax = lax.fori_loop(0, seq // tq, key_norm, jnp.zeros((1, tq), F32))
        kn_ref[...] = jnp.zeros((1, tq), F32) + jnp.max(kmax)

    zero_half = jnp.zeros((HEAD_DIM, tq), BF16)
    for c in range(ATT_WIDTH // LANES):
        t = q_ref[:, LANES * c:LANES * (c + 1)].astype(F32).T
        for half in range(2):
            j, g = divmod(2 * c + half, Q_PER_KV)
            lo = (j % 2) * HEAD_DIM
            cols = slice(tq * g, tq * (g + 1))
            qh_ref[j, lo:lo + HEAD_DIM, cols] = t[HEAD_DIM * half:HEAD_DIM * (half + 1), :].astype(BF16)
            qh_ref[j, HEAD_DIM - lo:2 * HEAD_DIM - lo, cols] = zero_half
    for c in range(IDX_HEADS * IDX_DIM // LANES):
        t = qi_ref[:, LANES * c:LANES * (c + 1)].astype(F32).T
        for half in range(2):
            h = 2 * c + half
            qih_ref[0:IDX_DIM, tq * h:tq * (h + 1)] = t[IDX_DIM * half:IDX_DIM * (half + 1), :].astype(BF16)
    qih_ref[IDX_DIM:LANES, :] = jnp.zeros((LANES - IDX_DIM, IDX_HEADS * tq), BF16)
    w8 = wi_ref[...].T[IDX_DIM:IDX_DIM + IDX_HEADS, :] * (IDX_HEADS ** -0.5) * (IDX_DIM ** -0.5)

    row = lax.broadcasted_iota(I32, (tq, tq), 0)
    col = lax.broadcasted_iota(I32, (tq, tq), 1)
    row_chunk = row >> 6
    col_chunk = col >> 6

    def tree(op, x):
        groups = [x[8 * r:8 * (r + 1)] for r in range(x.shape[0] // 8)]
        chains = groups[:4]
        for r in range(4, len(groups)):
            chains[r % 4] = op(chains[r % 4], groups[r])
        return op(op(chains[0], chains[1]), op(chains[2], chains[3]))

    def score_block(kb, carry):
        smax, smin = carry
        r0 = pl.multiple_of(kb * tq, tq)
        s = _dot(ki_ref[pl.ds(r0, tq), :], qih_ref[...])
        acc = jnp.maximum(s[:, 0:tq], 0.0) * w8[0:1, :]
        for h in range(1, IDX_HEADS):
            acc = acc + jnp.maximum(s[:, tq * h:tq * (h + 1)], 0.0) * w8[h:h + 1, :]
        acc = jnp.where(acc == 0.0, 0.0, acc)
        later = ((kb - i) * (tq // CHUNK) + row_chunk) > col_chunk
        sc_ref[pl.ds(r0, tq), :] = jnp.where(later, -jnp.inf, acc)
        smax = jnp.maximum(smax, tree(jnp.maximum, jnp.where(later, -jnp.inf, acc)))
        smin = jnp.minimum(smin, tree(jnp.minimum, jnp.where(later, jnp.inf, acc)))
        return smax, smin

    smax, smin = lax.fori_loop(0, nblk, score_block,
                               (jnp.full((8, tq), -jnp.inf, F32), jnp.full((8, tq), jnp.inf, F32)))
    smax = jnp.max(smax, axis=0, keepdims=True)
    smin = jnp.min(smin, axis=0, keepdims=True)

    def lane_pass(elem, op, init):
        def body(kb, acc):
            r0 = pl.multiple_of(kb * tq, tq)
            return op(acc, tree(op, elem(sc_ref[pl.ds(r0, tq), :], r0)))
        return lax.fori_loop(0, nblk, body, init)

    def count(pred):
        acc = lane_pass(lambda blk, r0: jnp.where(pred(blk, r0), 1, 0).astype(I32), jnp.add,
                        jnp.zeros((8, tq), I32))
        return jnp.sum(acc, axis=0, keepdims=True)

    def max_below(bound):
        acc = lane_pass(lambda blk, r0: jnp.where(blk < bound, blk, -jnp.inf), jnp.maximum,
                        jnp.full((8, tq), -jnp.inf, F32))
        return jnp.max(acc, axis=0, keepdims=True)

    def as_threshold(key):
        ks = key ^ INT_MIN
        return pltpu.bitcast(jnp.where(ks >= 0, ks, ks ^ np.int32(0x7FFFFFFF)), F32)

    def next_up(x):
        bits = pltpu.bitcast(x, I32)
        ks = jnp.where(bits >= 0, bits, bits ^ np.int32(0x7FFFFFFF)) + 1
        return pltpu.bitcast(jnp.where(ks >= 0, ks, ks ^ np.int32(0x7FFFFFFF)), F32)

    n_adm = (CHUNK * (tq // CHUNK) * i + CHUNK) + CHUNK * col_chunk[0:1, :]
    few = n_adm <= ksel

    def halve(_, st):
        lo, hi, c_lo, c_hi = st
        t = 0.5 * lo + 0.5 * hi
        inside = (t > lo) & (t < hi)
        t = jnp.where(inside, t, lo)
        c = count(lambda blk, r0: blk >= t)
        up = inside & (c >= ksel)
        dn = inside & (c < ksel)
        return jnp.where(up, t, lo), jnp.where(dn, t, hi), jnp.where(up, c, c_lo), jnp.where(dn, c, c_hi)

    zero = jnp.zeros((1, tq), I32)
    lo, hi, c_lo, c_hi = lax.fori_loop(
        0, BISECT_STEPS, halve, (smin, next_up(jnp.minimum(smax, F32_MAX)), n_adm, zero))

    def unresolved(c_lo, exact):
        return (c_lo > ksel) & (exact == 0) & jnp.logical_not(few)

    def peel_cond(st):
        it, lo, hi, c_lo, c_hi, exact = st
        return (it < PEEL_STEPS) & (jnp.max(jnp.where(unresolved(c_lo, exact), 1, 0)) > 0)

    def peel(st):
        it, lo, hi, c_lo, c_hi, exact = st
        v = max_below(hi)
        c_v = count(lambda blk, r0: blk >= v)
        act = unresolved(c_lo, exact)
        hit = act & (c_v >= ksel)
        miss = act & (c_v < ksel)
        return (it + 1, jnp.where(hit, v, lo), jnp.where(miss, v, hi), jnp.where(hit, c_v, c_lo),
                jnp.where(miss, c_v, c_hi), jnp.where(hit, 1, exact))

    _, lo, hi, c_lo, c_hi, exact = lax.while_loop(peel_cond, peel, (jnp.int32(0), lo, hi, c_lo, c_hi, zero))
    thr_ref[...] = jnp.where(few, -F32_MAX, lo)
    cnt_ref[0:1, :] = c_lo
    cnt_ref[1:2, :] = c_hi

    @pl.when(jnp.max(jnp.where(unresolved(c_lo, exact), 1, 0)) > 0)
    def _():
        def bisect(it, carry):
            t, cnt_t = carry
            cand = t | (jnp.int32(1) << (31 - it))
            cand_f = as_threshold(cand)
            c = count(lambda blk, r0: blk >= cand_f)
            ok = c >= ksel
            return jnp.where(ok, cand, t), jnp.where(ok, c, cnt_t)

        thr_key, cnt_ge = lax.fori_loop(0, 32, bisect, (zero, zero))
        t = as_threshold(thr_key)
        t = jnp.maximum(jnp.where(t != t, -F32_MAX, t), -F32_MAX)
        thr_ref[...] = t
        cnt_ref[0:1, :] = cnt_ge
        cnt_ref[1:2, :] = count(lambda blk, r0: blk > t)

    thr = thr_ref[...]
    cnt_ge = cnt_ref[0:1, :]
    need = ksel - cnt_ref[1:2, :]

    jbits = int(seq).bit_length()
    jcut_ref[...] = jnp.full((1, tq), (1 << jbits) - 1, I32)

    @pl.when(jnp.max(jnp.where(cnt_ge > ksel, 1, 0)) > 0)
    def _():
        def jbisect(it, jc):
            cand = jc | (jnp.int32(1) << (jbits - 1 - it))
            f = count(lambda blk, r0: (blk == thr) & ((r0 + row) < cand))
            return jnp.where(f <= need, cand, jc)
        jcut_ref[...] = lax.fori_loop(0, jbits, jbisect, jnp.zeros((1, tq), I32))

    jcut = jcut_ref[...]

    acc_ref[...] = jnp.zeros(acc_ref.shape, F32)

    def selected(kb):
        r0 = pl.multiple_of(kb * tq, tq)
        blk = sc_ref[pl.ds(r0, tq), :]
        return r0, (blk > thr) | ((blk == thr) & ((r0 + row) < jcut))

    qn2 = None
    for j in range(N_KV_HEADS):
        x = qh_ref[j].astype(F32)
        ssq = jnp.sum(x * x, axis=0, keepdims=True)
        for g in range(Q_PER_KV):
            part = ssq[:, tq * g:tq * (g + 1)]
            qn2 = part if qn2 is None else jnp.maximum(qn2, part)
    bound = SHIFT_MARGIN * jnp.sqrt(qn2 * kn_ref[...])
    fixed_shift_ok = jnp.max(bound) <= SHIFT_MAX

    @pl.when(fixed_shift_ok)
    def _():
        def probabilities(kb):
            r0, sel = selected(kb)
            shift = jnp.where(sel, bound, -jnp.inf)
            shift = jnp.concatenate([shift] * Q_PER_KV, axis=1)
            for j in range(N_KV_HEADS):
                s = _dot(k_ref[pl.ds(r0, tq), LANES * (j // 2):LANES * (j // 2 + 1)], qh_ref[j])
                p_ref[j] = jnp.exp2(s + shift).astype(BF16)

        def accumulate(kb):
            for j in range(N_KV_HEADS):
                acc_ref[j] += _dot(vt_ref[kb, j], p_ref[j])

        def attend(kb, _):
            accumulate(kb)
            probabilities(kb + 1)
            return 0

        probabilities(0)
        lax.fori_loop(0, nblk - 1, attend, 0)
        accumulate(nblk - 1)

    @pl.when(jnp.logical_not(fixed_shift_ok))
    def _():
        _online_softmax_attend(k_ref, vt_ref, qh_ref, acc_ref, m_ref, s_ref, mb_ref, selected, nblk, tq)

    slabs = []
    for pair in range(N_HEADS // 2):
        j, g0 = pair // 2, 2 * (pair % 2)
        inv_l = 1.0 / acc_ref[j, HEAD_DIM:HEAD_DIM + 1, tq * g0:tq * (g0 + 2)]
        o2 = acc_ref[j, 0:HEAD_DIM, tq * g0:tq * (g0 + 2)] * inv_l
        slabs.append(jnp.concatenate([o2[:, 0:tq], o2[:, tq:2 * tq]], axis=0).T)
    att = jnp.concatenate(slabs, axis=1)
    y = _dot((att * gate_ref[...].astype(F32)).astype(BF16), wout_ref[...])
    hh = h1_ref[...] + y
    ms = jnp.mean(hh * hh, axis=-1, keepdims=True)
    o_ref[0] = hh * lax.rsqrt(ms + NORM_EPS) * fg_ref[...]


def _online_softmax_attend(k_ref, vt_ref, qh_ref, acc_ref, m_ref, s_ref, mb_ref, selected, nblk, tq):
    m_ref[...] = jnp.full(m_ref.shape, NEG_BIG, F32)

    def logits(kb, slot):
        r0, sel = selected(kb)
        bias = jnp.where(sel, 0.0, -jnp.inf)
        bias = jnp.concatenate([bias] * Q_PER_KV, axis=1)
        for j in range(N_KV_HEADS):
            s = _dot(k_ref[pl.ds(r0, tq), LANES * (j // 2):LANES * (j // 2 + 1)], qh_ref[j]) + bias
            s_ref[slot, j] = s
            mb_ref[slot, j:j + 1, :] = jnp.max(s, axis=0, keepdims=True)

    def accumulate(kb, slot):
        for j in range(N_KV_HEADS):
            m_old = m_ref[j:j + 1, :]
            m_new = jnp.maximum(m_old, mb_ref[slot, j:j + 1, :])
            m_ref[j:j + 1, :] = m_new
            p = jnp.exp2(s_ref[slot, j] - m_new).astype(BF16)
            acc_ref[j] = acc_ref[j] * jnp.exp2(m_old - m_new) + _dot(vt_ref[kb, j], p)

    def attend_pair(t, _):
        kb = 2 * t
        logits(kb + 1, 1)
        accumulate(kb, 0)
        logits(kb + 2, 0)
        accumulate(kb + 1, 1)
        return 0

    logits(0, 0)
    npair = (nblk - 1) // 2
    lax.fori_loop(0, npair, attend_pair, 0)
    done = 2 * npair

    @pl.when(nblk - done == 1)
    def _():
        accumulate(done, 0)

    @pl.when(nblk - done == 2)
    def _():
        logits(done + 1, 1)
        accumulate(done, 0)
        accumulate(done + 1, 1)


def _attention(q, qi, wi, gate, h1, k, v, ki, w_out, final_g, *, seq, ksel):
    tq = ATT_TILE
    ntile = seq // tq
    tile = lambda w_: pl.BlockSpec((tq, w_), lambda b, i: (b * ntile + i, 0))
    whole = lambda w_: pl.BlockSpec((seq, w_), lambda b, i: (b, 0))
    const = lambda shape: pl.BlockSpec(shape, lambda b, i: (0,) * len(shape))
    return pl.pallas_call(
        functools.partial(_attn_kernel, seq=seq, ksel=ksel),
        grid=(NBATCH, seq // tq),
        in_specs=[tile(ATT_WIDTH), tile(IDX_HEADS * IDX_DIM), tile(LANES), tile(ATT_WIDTH), tile(D_MODEL),
                  whole(KV_WIDTH), whole(KV_WIDTH), whole(LANES),
                  const((ATT_WIDTH, D_MODEL)), const((1, D_MODEL))],
        out_specs=pl.BlockSpec((1, tq, D_MODEL), lambda b, i: (b, i, 0)),
        out_shape=jax.ShapeDtypeStruct((NBATCH, seq, D_MODEL), F32),
        scratch_shapes=[
            pltpu.VMEM((seq // tq, N_KV_HEADS, PV_ROWS, tq), BF16),
            pltpu.VMEM((seq, tq), F32),
            pltpu.VMEM((N_KV_HEADS, LANES, Q_PER_KV * tq), BF16),
            pltpu.VMEM((LANES, IDX_HEADS * tq), BF16),
            pltpu.VMEM((N_KV_HEADS, PV_ROWS, Q_PER_KV * tq), F32),
            pltpu.VMEM((N_KV_HEADS, Q_PER_KV * tq), F32),
            pltpu.VMEM((1, tq), I32),
            pltpu.VMEM((2, N_KV_HEADS, tq, Q_PER_KV * tq), F32),
            pltpu.VMEM((2, N_KV_HEADS, Q_PER_KV * tq), F32),
            pltpu.VMEM((1, tq), F32),
            pltpu.VMEM((2, tq), I32),
            pltpu.VMEM((1, tq), F32),
            pltpu.VMEM((N_KV_HEADS, tq, Q_PER_KV * tq), BF16),
        ],
        compiler_params=pltpu.CompilerParams(dimension_semantics=("arbitrary", "arbitrary"),
                                             vmem_limit_bytes=VMEM_LIMIT),
        name="dsa_attention",
    )(q, qi, wi, gate, h1, k, v, ki, w_out, final_g)


def _rope_tables(seq):
    pos = jnp.arange(seq, dtype=F32)
    inv = ROPE_THETA ** (-jnp.arange(0, ROT_DIM, 2, dtype=F32) / ROT_DIM)
    ang = pos[:, None] * inv[None, :]
    cos, sin = jnp.cos(ang), jnp.sin(ang)
    ones = jnp.ones((seq, HEAD_DIM - ROT_DIM), F32)
    return (jnp.concatenate([cos, cos, ones, cos, cos, ones], axis=1),
            jnp.concatenate([-sin, sin, 0.0 * ones, -sin, sin, 0.0 * ones], axis=1))


def _reorder_odd_weights(w):
    o = np.cumsum([0, ATT_WIDTH, ATT_WIDTH, KV_WIDTH, KV_WIDTH, IDX_HEADS * IDX_DIM, IDX_DIM, IDX_HEADS]).tolist()
    q, z, k, v, qi, ki, wi = [w[:, o[j]:o[j + 1]] for j in range(7)]
    pad = jnp.zeros((w.shape[0], LANES - IDX_DIM - IDX_HEADS), w.dtype)
    return jnp.concatenate([q, k, qi, ki, wi, pad, z, v], axis=1)


def kernel(x, norm_g, e_w_in, e_lam_re, e_lam_im, e_log_step, e_b_re, e_b_im, e_c_re, e_c_im, e_d_skip,
           e_w_glu, e_b_glu, e_conv_w, e_conv_b, e_ln_g, e_ln_b, e_w_out, o_w_in, o_w_out, final_g):
    bsz, seq, dm = x.shape
    assert bsz == NBATCH and dm == D_MODEL and seq % ATT_TILE == 0
    assert norm_g.shape[0] == 2 and e_w_in.shape[0] == 1 and o_w_in.shape[0] == 1
    tt = 64
    ksel = min(TOPK_MAX, seq // 4)
    row = lambda a: a.reshape(1, -1).astype(F32)

    ab_re, ab_im, bb_re, bb_im = _s5_prep(e_lam_re[0], e_lam_im[0], e_log_step[0], e_b_re[0], e_b_im[0])
    bc, cc = _s5_block_weights(bb_re, bb_im, e_c_re[0], e_c_im[0])
    a_re = jnp.broadcast_to(ab_re.reshape(1, S5_NSTATE), (NBATCH, S5_NSTATE))
    a_im = jnp.broadcast_to(ab_im.reshape(1, S5_NSTATE), (NBATCH, S5_NSTATE))
    conv_w = jnp.broadcast_to(e_conv_w[0][:, None, :], (CONV_KERNEL, NBATCH, CONV_WIDTH))

    h1 = _layer0(x.reshape(NBATCH, seq * D_MODEL), row(norm_g[0]), e_w_in[0].astype(BF16), a_re, a_im, bc, cc,
                 row(e_d_skip[0]), e_w_glu[0].astype(BF16), row(e_b_glu[0]), conv_w, row(e_conv_b[0]),
                 row(e_ln_g[0]), row(e_ln_b[0]), e_w_out[0].astype(BF16), seq=seq, tt=tt)
    h1 = h1.reshape(NBATCH * seq, D_MODEL)

    cos_t, sin_t = _rope_tables(seq)
    q, k, qi, ki, wi, gate, v = _inproj1(h1, row(norm_g[1]), _reorder_odd_weights(o_w_in[0]).astype(BF16),
                                         cos_t, sin_t, seq=seq, rows=512)
    return _attention(q, qi, wi, gate, h1, k, v, ki, o_w_out[0].astype(BF16), row(final_g), seq=seq, ksel=ksel)
```

```python
import functools
import math

import numpy as np
import jax
import jax.numpy as jnp
from jax import lax
from jax.experimental import pallas as pl
from jax.experimental.pallas import tpu as pltpu

F32 = jnp.float32
BF16 = jnp.bfloat16
I32 = jnp.int32

NBATCH = 8
D_MODEL = 1024
CHUNK = 64
NORM_EPS = 1e-6
S5_WIDTH = 512
S5_GROUP = 16
S5_GROUPS = 32
S5_STATE = 64
S5_NSTATE = S5_GROUPS * S5_STATE
S5_SLABS = 4
CONV_WIDTH = 512
CONV_KERNEL = 31
CONV_HALO = (CONV_KERNEL - 1) * NBATCH
EVEN_IN = 2 * S5_WIDTH + 3 * CONV_WIDTH
N_HEADS = 16
HEAD_DIM = 64
N_KV_HEADS = 4
Q_PER_KV = 4
ATT_WIDTH = 1024
KV_WIDTH = 256
IDX_HEADS = 8
IDX_DIM = 64
TOPK_MAX = 256
ROPE_THETA = 500000.0
ROT_DIM = 16
LANES = 128
ODD_PAD = 3200
ATT_TILE = 256
PV_ROWS = HEAD_DIM + 16
BISECT_STEPS = 16
PEEL_STEPS = 6
SHIFT_MARGIN = 1.02
SHIFT_MAX = 50.0
INT_MIN = np.int32(-2 ** 31)
F32_MAX = float(np.finfo(np.float32).max)
NEG_BIG = -1e38
LOG2E = math.log2(math.e)

VMEM_LIMIT = 56 * 1024 * 1024


def _dot(a, b):
    return jnp.dot(a, b, preferred_element_type=F32)


def _dot_nt(a, b):
    return lax.dot_general(a, b, (((1,), (1,)), ((), ())), preferred_element_type=F32)


def _sigmoid(x):
    return 1.0 / (1.0 + jnp.exp(-x))


def _silu(x):
    return x * _sigmoid(x)


def _gelu_tanh(x):
    c = math.sqrt(2.0 / math.pi)
    return 0.5 * x * (1.0 + jnp.tanh(c * (x + 0.044715 * (x * x * x))))


def _s5_prep_kernel(lr_ref, li_ref, ls_ref, br_ref, bi_ref, abr_ref, abi_ref, bbr_ref, bbi_ref):
    lr = jnp.minimum(lr_ref[...], -1e-4)
    li = li_ref[...]
    dt = jnp.exp(ls_ref[...])
    mag = jnp.exp(lr * dt)
    ab_re = mag * jnp.cos(li * dt)
    ab_im = mag * jnp.sin(li * dt)
    den = lr * lr + li * li
    nr = ab_re - 1.0
    ni = ab_im
    k_re = (nr * lr + ni * li) / den
    k_im = (ni * lr - nr * li) / den
    br = br_ref[...]
    bi = bi_ref[...]
    abr_ref[...] = ab_re
    abi_ref[...] = ab_im
    bbr_ref[...] = k_re * br - k_im * bi
    bbi_ref[...] = k_re * bi + k_im * br


def _s5_prep(lam_re, lam_im, log_step, b_re, b_im):
    g, n, c = b_re.shape
    shape2 = (g * n * c // LANES, LANES)
    expand = lambda a: jnp.broadcast_to(a[:, :, None], (g, n, c)).reshape(shape2)
    ls = jnp.broadcast_to(log_step[:, None], (g, n))
    outs = pl.pallas_call(
        _s5_prep_kernel,
        out_shape=[jax.ShapeDtypeStruct(shape2, F32)] * 4,
        name="s5_prep",
    )(expand(lam_re), expand(lam_im), expand(ls), b_re.reshape(shape2), b_im.reshape(shape2))
    ab_re, ab_im, bb_re, bb_im = [o.reshape(g, n, c) for o in outs]
    return ab_re[:, :, 0], ab_im[:, :, 0], bb_re, bb_im


def _s5_block_weights(bb_re, bb_im, c_re, c_im):
    gl = S5_GROUPS // S5_SLABS
    eye = jnp.eye(gl, dtype=F32)

    def in_map(bb):
        t = bb.reshape(S5_SLABS, gl, S5_STATE, S5_GROUP).transpose(0, 1, 3, 2)
        t = t[:, :, :, None, :] * eye[None, :, None, :, None]
        return t.reshape(S5_SLABS, gl * S5_GROUP, gl * S5_STATE)

    def out_map(cc):
        t = cc.reshape(S5_SLABS, gl, S5_GROUP, S5_STATE).transpose(0, 1, 3, 2)
        t = t[:, :, :, None, :] * eye[None, :, None, :, None]
        return t.reshape(S5_SLABS, gl * S5_STATE, gl * S5_GROUP)

    bc = jnp.concatenate([in_map(bb_re), in_map(bb_im)], axis=2).astype(BF16)
    cc = jnp.concatenate([out_map(c_re), out_map(-c_im)], axis=1).astype(BF16)
    return bc, cc


def _layer0_kernel(x_ref, g_ref, win_ref, are_ref, aim_ref, bc_ref, cc_ref, dsk_ref, wglu_ref, bglu_ref,
                   cw_ref, cb_ref, lng_ref, lnb_ref, wout_ref, o_ref,
                   hn_ref, proj_ref, xs_ref, st_ref, hc_ref, gl_ref, ycat_ref, tm_ref, *, tt):
    rows = NBATCH * tt
    half = S5_NSTATE
    slab_w = half // S5_SLABS

    @pl.when(pl.program_id(0) == 0)
    def _():
        st_ref[...] = jnp.zeros_like(st_ref)
        hc_ref[0:CONV_HALO, :] = jnp.zeros((CONV_HALO, CONV_WIDTH), F32)

    gain = g_ref[...]
    for tp in range(tt // 2):
        xa = x_ref[:, (2 * tp) * D_MODEL:(2 * tp + 1) * D_MODEL]
        xb = x_ref[:, (2 * tp + 1) * D_MODEL:(2 * tp + 2) * D_MODEL]
        xx = jnp.concatenate([xa, xb], axis=0)
        ms = jnp.mean(xx * xx, axis=-1, keepdims=True)
        hn_ref[16 * tp:16 * tp + 16, :] = (xx * lax.rsqrt(ms + NORM_EPS) * gain).astype(BF16)

    proj_ref[...] = _dot(hn_ref[...], win_ref[...])

    for m in range(S5_SLABS):
        u_m = proj_ref[:, LANES * m:LANES * (m + 1)].astype(BF16)
        xm = _dot(u_m, bc_ref[m])
        xs_ref[:, slab_w * m:slab_w * (m + 1)] = xm[:, :slab_w]
        xs_ref[:, half + slab_w * m:half + slab_w * (m + 1)] = xm[:, slab_w:]

    cw = 512
    for cc_i in range(half // cw):
        lo = cc_i * cw
        a_re = are_ref[:, lo:lo + cw]
        a_im = aim_ref[:, lo:lo + cw]

        def step(t, carry, lo=lo, a_re=a_re, a_im=a_im):
            s_re, s_im = carry
            r0 = pl.multiple_of(t * NBATCH, NBATCH)
            x_re = xs_ref[pl.ds(r0, NBATCH), lo:lo + cw]
            x_im = xs_ref[pl.ds(r0, NBATCH), half + lo:half + lo + cw]
            n_re = a_re * s_re - a_im * s_im + x_re
            n_im = a_re * s_im + a_im * s_re + x_im
            xs_ref[pl.ds(r0, NBATCH), lo:lo + cw] = n_re
            xs_ref[pl.ds(r0, NBATCH), half + lo:half + lo + cw] = n_im
            return n_re, n_im

        s_re, s_im = lax.fori_loop(0, tt, step, (st_ref[:, lo:lo + cw], st_ref[:, half + lo:half + lo + cw]),
                                   unroll=True)
        st_ref[:, lo:lo + cw] = s_re
        st_ref[:, half + lo:half + lo + cw] = s_im

    for m in range(S5_SLABS):
        s_re = xs_ref[:, slab_w * m:slab_w * (m + 1)].astype(BF16)
        s_im = xs_ref[:, half + slab_w * m:half + slab_w * (m + 1)].astype(BF16)
        y_m = _dot(s_re, cc_ref[m, 0:slab_w, :]) + _dot(s_im, cc_ref[m, slab_w:2 * slab_w, :])
        y_m = y_m + proj_ref[:, LANES * m:LANES * (m + 1)] * dsk_ref[:, LANES * m:LANES * (m + 1)]
        gl_ref[:, LANES * m:LANES * (m + 1)] = _gelu_tanh(y_m)
    gl = gl_ref[...]
    out_a = gl * _sigmoid(_dot(gl.astype(BF16), wglu_ref[...]) + bglu_ref[...])
    ycat_ref[:, 0:S5_WIDTH] = (out_a * _silu(proj_ref[:, S5_WIDTH:2 * S5_WIDTH])).astype(BF16)

    o = 2 * S5_WIDTH
    hc_ref[CONV_HALO:CONV_HALO + rows, :] = (
        proj_ref[:, o:o + CONV_WIDTH] * _sigmoid(proj_ref[:, o + CONV_WIDTH:o + 2 * CONV_WIDTH]))
    rc = 64
    for cl in range(CONV_WIDTH // LANES):
        lanes = slice(LANES * cl, LANES * (cl + 1))
        taps = [cw_ref[k, :, lanes] for k in range(CONV_KERNEL)]

        def conv_chunk(c, _, lanes=lanes, taps=taps):
            r0 = pl.multiple_of(c * rc, rc)
            accs = [None] * (rc // NBATCH)
            for t in range(rc // NBATCH + CONV_KERNEL - 1):
                h = hc_ref[pl.ds(r0 + NBATCH * t, NBATCH), lanes]
                for rr in range(rc // NBATCH):
                    k = t - rr
                    if 0 <= k < CONV_KERNEL:
                        accs[rr] = h * taps[k] if accs[rr] is None else accs[rr] + h * taps[k]
            gl_ref[pl.ds(r0, rc), lanes] = jnp.concatenate(accs, axis=0)
            return 0

        lax.fori_loop(0, rows // rc, conv_chunk, 0)
    hc_ref[0:CONV_HALO, :] = hc_ref[rows:rows + CONV_HALO, :]
    h = gl_ref[...] + cb_ref[...]
    mu = jnp.mean(h, axis=-1, keepdims=True)
    var = jnp.mean(jnp.square(h - mu), axis=-1, keepdims=True)
    hf = (h - mu) * lax.rsqrt(var + NORM_EPS) * lng_ref[...] + lnb_ref[...]
    z_b = proj_ref[:, o + 2 * CONV_WIDTH:o + 3 * CONV_WIDTH]
    ycat_ref[:, S5_WIDTH:S5_WIDTH + CONV_WIDTH] = (_silu(hf) * _silu(z_b)).astype(BF16)

    y = _dot(ycat_ref[...], wout_ref[...])
    for t in range(tt):
        h = y[NBATCH * t:NBATCH * (t + 1), :] + x_ref[:, t * D_MODEL:(t + 1) * D_MODEL]
        for c in range(D_MODEL // LANES):
            tm_ref[c, NBATCH * t:NBATCH * (t + 1), :] = h[:, LANES * c:LANES * (c + 1)]
    for b in range(NBATCH):
        for c in range(D_MODEL // LANES):
            o_ref[b, :, LANES * c:LANES * (c + 1)] = tm_ref[c, pl.ds(b, tt, stride=NBATCH), :]


def _layer0(x2, gain, w_in, a_re, a_im, bc, cc, d_skip, w_glu, b_glu, conv_w, conv_b, ln_g, ln_b, w_out, *, seq, tt):
    rows = NBATCH * tt
    const = lambda shape: pl.BlockSpec(shape, lambda i: (0,) * len(shape))
    return pl.pallas_call(
        functools.partial(_layer0_kernel, tt=tt),
        grid=(seq // tt,),
        in_specs=[
            pl.BlockSpec((NBATCH, tt * D_MODEL), lambda i: (0, i)),
            const((1, D_MODEL)),
            const((D_MODEL, EVEN_IN)),
            const((NBATCH, S5_NSTATE)),
            const((NBATCH, S5_NSTATE)),
            const((S5_SLABS, LANES, 2 * S5_NSTATE // S5_SLABS)),
            const((S5_SLABS, 2 * S5_NSTATE // S5_SLABS, LANES)),
            const((1, S5_WIDTH)),
            const((S5_WIDTH, S5_WIDTH)),
            const((1, S5_WIDTH)),
            const((CONV_KERNEL, NBATCH, CONV_WIDTH)),
            const((1, CONV_WIDTH)),
            const((1, CONV_WIDTH)),
            const((1, CONV_WIDTH)),
            const((S5_WIDTH + CONV_WIDTH, D_MODEL)),
        ],
        out_specs=pl.BlockSpec((NBATCH, tt, D_MODEL), lambda i: (0, i, 0)),
        out_shape=jax.ShapeDtypeStruct((NBATCH, seq, D_MODEL), F32),
        scratch_shapes=[
            pltpu.VMEM((rows, D_MODEL), BF16),
            pltpu.VMEM((rows, EVEN_IN), F32),
            pltpu.VMEM((rows, 2 * S5_NSTATE), F32),
            pltpu.VMEM((NBATCH, 2 * S5_NSTATE), F32),
            pltpu.VMEM((CONV_HALO + rows, CONV_WIDTH), F32),
            pltpu.VMEM((rows, S5_WIDTH), F32),
            pltpu.VMEM((rows, S5_WIDTH + CONV_WIDTH), BF16),
            pltpu.VMEM((D_MODEL // LANES, rows, LANES), F32),
        ],
        compiler_params=pltpu.CompilerParams(dimension_semantics=("arbitrary",), vmem_limit_bytes=VMEM_LIMIT),
        name="layer0",
    )(x2, gain, w_in, a_re, a_im, bc, cc, d_skip, w_glu, b_glu, conv_w, conv_b, ln_g, ln_b, w_out)


def _inproj1_kernel(h_ref, g_ref, w_ref, cos_ref, sin_ref,
                    q_ref, k_ref, qi_ref, ki_ref, wi_ref, gate_ref, v_ref):
    h = h_ref[...]
    ms = jnp.mean(h * h, axis=-1, keepdims=True)
    hn = (h * lax.rsqrt(ms + NORM_EPS) * g_ref[...]).astype(BF16)
    proj = _dot(hn, w_ref[...])
    cos_t = cos_ref[...]
    sin_t = sin_ref[...]
    lane = lax.broadcasted_iota(I32, cos_t.shape, 1)
    first = (lane & (HEAD_DIM - 1)) < (ROT_DIM // 2)
    low = lane < HEAD_DIM

    def slab(c):
        return proj[:, LANES * c:LANES * (c + 1)]

    def rope(xs, ct, st):
        partner = jnp.where(first, pltpu.roll(xs, LANES - ROT_DIM // 2, 1), pltpu.roll(xs, ROT_DIM // 2, 1))
        return xs * ct + partner * st

    q_scale = (HEAD_DIM ** -0.5) * LOG2E
    for c in range(8):
        q_ref[:, LANES * c:LANES * (c + 1)] = (rope(slab(c), cos_t, sin_t) * q_scale).astype(BF16)
    for c in range(2):
        k_ref[:, LANES * c:LANES * (c + 1)] = rope(slab(8 + c), cos_t, sin_t).astype(BF16)
    for c in range(4):
        qi_ref[:, LANES * c:LANES * (c + 1)] = rope(slab(10 + c), cos_t, sin_t).astype(BF16)
    kiw = rope(slab(14), jnp.where(low, cos_t, 1.0), jnp.where(low, sin_t, 0.0))
    ki_ref[...] = kiw.astype(BF16)
    wi_ref[...] = kiw
    for c in range(8):
        gate_ref[:, LANES * c:LANES * (c + 1)] = _silu(slab(15 + c)).astype(BF16)
    for c in range(2):
        v_ref[:, LANES * c:LANES * (c + 1)] = slab(23 + c).astype(BF16)


def _inproj1(h1, gain, w, cos_t, sin_t, *, seq, rows):
    n = seq * NBATCH
    const = lambda shape: pl.BlockSpec(shape, lambda i: (0,) * len(shape))
    rb = lambda w_: pl.BlockSpec((rows, w_), lambda i: (i, 0))
    pos = pl.BlockSpec((rows, LANES), lambda i: (i % (seq // rows), 0))
    return pl.pallas_call(
        _inproj1_kernel,
        grid=(n // rows,),
        in_specs=[rb(D_MODEL), const((1, D_MODEL)), const((D_MODEL, ODD_PAD)), pos, pos],
        out_specs=[rb(ATT_WIDTH), rb(KV_WIDTH), rb(IDX_HEADS * IDX_DIM), rb(LANES), rb(LANES), rb(ATT_WIDTH),
                   rb(KV_WIDTH)],
        out_shape=[
            jax.ShapeDtypeStruct((n, ATT_WIDTH), BF16),
            jax.ShapeDtypeStruct((n, KV_WIDTH), BF16),
            jax.ShapeDtypeStruct((n, IDX_HEADS * IDX_DIM), BF16),
            jax.ShapeDtypeStruct((n, LANES), BF16),
            jax.ShapeDtypeStruct((n, LANES), F32),
            jax.ShapeDtypeStruct((n, ATT_WIDTH), BF16),
            jax.ShapeDtypeStruct((n, KV_WIDTH), BF16),
        ],
        compiler_params=pltpu.CompilerParams(dimension_semantics=("arbitrary",), vmem_limit_bytes=VMEM_LIMIT),
        name="inproj1",
    )(h1, gain, w, cos_t, sin_t)


def _attn_kernel(q_ref, qi_ref, wi_ref, gate_ref, h1_ref, k_ref, v_ref, ki_ref, wout_ref, fg_ref, o_ref,
                 vt_ref, sc_ref, qh_ref, qih_ref, acc_ref, m_ref, jcut_ref, s_ref, mb_ref, thr_ref, cnt_ref,
                 kn_ref, p_ref, *, seq, ksel):
    tq = ATT_TILE
    i = pl.program_id(1)
    nblk = i + 1

    @pl.when(i == 0)
    def _():
        def transpose_v(kb, _):
            r0 = pl.multiple_of(kb * tq, tq)
            vt = v_ref[pl.ds(r0, tq), :].astype(F32).T
            for j in range(N_KV_HEADS):
                vt_ref[kb, j, 0:HEAD_DIM, :] = vt[HEAD_DIM * j:HEAD_DIM * (j + 1), :].astype(BF16)
                vt_ref[kb, j, HEAD_DIM:PV_ROWS, :] = jnp.ones((PV_ROWS - HEAD_DIM, tq), BF16)
            return 0
        lax.fori_loop(0, seq // tq, transpose_v, 0)

        def key_norm(kb, mx):
            r0 = pl.multiple_of(kb * tq, tq)
            kt = k_ref[pl.ds(r0, tq), :].astype(F32).T
            sq = kt * kt
            for j in range(N_KV_HEADS):
                mx = jnp.maximum(mx, jnp.sum(sq[HEAD_DIM * j:HEAD_DIM * (j + 1), :], axis=0, keepdims=True))
            return mx
        kmax = lax.fori_loop(0, seq // tq, key_norm, jnp.zeros((1, tq), F32))
        kn_ref[...] = jnp.zeros((1, tq), F32) + jnp.max(kmax)

    zero_half = jnp.zeros((HEAD_DIM, tq), BF16)
    for c in range(ATT_WIDTH // LANES):
        t = q_ref[:, LANES * c:LANES * (c + 1)].astype(F32).T
        for half in range(2):
            j, g = divmod(2 * c + half, Q_PER_KV)
            lo = (j % 2) * HEAD_DIM
            cols = slice(tq * g, tq * (g + 1))
            qh_ref[j, lo:lo + HEAD_DIM, cols] = t[HEAD_DIM * half:HEAD_DIM * (half + 1), :].astype(BF16)
            qh_ref[j, HEAD_DIM - lo:2 * HEAD_DIM - lo, cols] = zero_half
    for c in range(IDX_HEADS * IDX_DIM // LANES):
        t = qi_ref[:, LANES * c:LANES * (c + 1)].astype(F32).T
        for half in range(2):
            h = 2 * c + half
            qih_ref[0:IDX_DIM, tq * h:tq * (h + 1)] = t[IDX_DIM * half:IDX_DIM * (half + 1), :].astype(BF16)
    qih_ref[IDX_DIM:LANES, :] = jnp.zeros((LANES - IDX_DIM, IDX_HEADS * tq), BF16)
    w8 = wi_ref[...].T[IDX_DIM:IDX_DIM + IDX_HEADS, :] * (IDX_HEADS ** -0.5) * (IDX_DIM ** -0.5)

    row = lax.broadcasted_iota(I32, (tq, tq), 0)
    col = lax.broadcasted_iota(I32, (tq, tq), 1)
    row_chunk = row >> 6
    col_chunk = col >> 6

    def tree(op, x):
        groups = [x[8 * r:8 * (r + 1)] for r in range(x.shape[0] // 8)]
        chains = groups[:4]
        for r in range(4, len(groups)):
            chains[r % 4] = op(chains[r % 4], groups[r])
        return op(op(chains[0], chains[1]), op(chains[2], chains[3]))

    half_heads = IDX_HEADS // 2

    def index_logits(kb):
        ki = ki_ref[pl.ds(pl.multiple_of(kb * tq, tq), tq), :]
        for part in range(2):
            s_ref[0, part] = _dot(ki, qih_ref[:, part * half_heads * tq:(part + 1) * half_heads * tq])

    def score_block(kb, carry):
        smax, smin = carry
        r0 = pl.multiple_of(kb * tq, tq)
        acc = None
        for h in range(IDX_HEADS):
            sh = s_ref[0, h // half_heads, :, tq * (h % half_heads):tq * (h % half_heads + 1)]
            term = jnp.maximum(sh, 0.0) * w8[h:h + 1, :]
            acc = term if acc is None else acc + term
        acc = jnp.where(acc == 0.0, 0.0, acc)
        later = ((kb - i) * (tq // CHUNK) + row_chunk) > col_chunk
        sc_ref[pl.ds(r0, tq), :] = jnp.where(later, -jnp.inf, acc)
        smax = jnp.maximum(smax, tree(jnp.maximum, jnp.where(later, -jnp.inf, acc)))
        smin = jnp.minimum(smin, tree(jnp.minimum, jnp.where(later, jnp.inf, acc)))
        return smax, smin

    def score_step(kb, carry):
        carry = score_block(kb, carry)
        index_logits(kb + 1)
        return carry

    index_logits(0)
    carry = lax.fori_loop(0, nblk - 1, score_step,
                          (jnp.full((8, tq), -jnp.inf, F32), jnp.full((8, tq), jnp.inf, F32)))
    smax, smin = score_block(nblk - 1, carry)
    smax = jnp.max(smax, axis=0, keepdims=True)
    smin = jnp.min(smin, axis=0, keepdims=True)

    @pl.when(nblk % 2 == 1)
    def _():
        sc_ref[pl.ds(pl.multiple_of(nblk * tq, tq), tq), :] = jnp.full((tq, tq), -jnp.inf, F32)

    def lane_pass(elem, op, init):
        def body(kb2, acc):
            r0 = pl.multiple_of(kb2 * 2 * tq, 2 * tq)
            acc = op(acc, tree(op, elem(sc_ref[pl.ds(r0, tq), :], r0)))
            return op(acc, tree(op, elem(sc_ref[pl.ds(r0 + tq, tq), :], r0 + tq)))
        return lax.fori_loop(0, (nblk + 1) // 2, body, init)

    def count(pred):
        acc = lane_pass(lambda blk, r0: jnp.where(pred(blk, r0), 1, 0).astype(I32), jnp.add,
                        jnp.zeros((8, tq), I32))
        return jnp.sum(acc, axis=0, keepdims=True)

    def max_below(bound):
        acc = lane_pass(lambda blk, r0: jnp.where(blk < bound, blk, -jnp.inf), jnp.maximum,
                        jnp.full((8, tq), -jnp.inf, F32))
        return jnp.max(acc, axis=0, keepdims=True)

    def as_threshold(key):
        ks = key ^ INT_MIN
        return pltpu.bitcast(jnp.where(ks >= 0, ks, ks ^ np.int32(0x7FFFFFFF)), F32)

    def next_up(x):
        bits = pltpu.bitcast(x, I32)
        ks = jnp.where(bits >= 0, bits, bits ^ np.int32(0x7FFFFFFF)) + 1
        return pltpu.bitcast(jnp.where(ks >= 0, ks, ks ^ np.int32(0x7FFFFFFF)), F32)

    n_adm = (CHUNK * (tq // CHUNK) * i + CHUNK) + CHUNK * col_chunk[0:1, :]
    few = n_adm <= ksel

    def halve(_, st):
        lo, hi, c_lo, c_hi = st
        t = 0.5 * lo + 0.5 * hi
        inside = (t > lo) & (t < hi)
        t = jnp.where(inside, t, lo)
        c = count(lambda blk, r0: blk >= t)
        up = inside & (c >= ksel)
        dn = inside & (c < ksel)
        return jnp.where(up, t, lo), jnp.where(dn, t, hi), jnp.where(up, c, c_lo), jnp.where(dn, c, c_hi)

    zero = jnp.zeros((1, tq), I32)
    lo, hi, c_lo, c_hi = lax.fori_loop(
        0, BISECT_STEPS, halve, (smin, next_up(jnp.minimum(smax, F32_MAX)), n_adm, zero))

    def unresolved(c_lo, exact):
        return (c_lo > ksel) & (exact == 0) & jnp.logical_not(few)

    def peel_cond(st):
        it, lo, hi, c_lo, c_hi, exact = st
        return (it < PEEL_STEPS) & (jnp.max(jnp.where(unresolved(c_lo, exact), 1, 0)) > 0)

    def peel(st):
        it, lo, hi, c_lo, c_hi, exact = st
        v = max_below(hi)
        c_v = count(lambda blk, r0: blk >= v)
        act = unresolved(c_lo, exact)
        hit = act & (c_v >= ksel)
        miss = act & (c_v < ksel)
        return (it + 1, jnp.where(hit, v, lo), jnp.where(miss, v, hi), jnp.where(hit, c_v, c_lo),
                jnp.where(miss, c_v, c_hi), jnp.where(hit, 1, exact))

    _, lo, hi, c_lo, c_hi, exact = lax.while_loop(peel_cond, peel, (jnp.int32(0), lo, hi, c_lo, c_hi, zero))
    thr_ref[...] = jnp.where(few, -F32_MAX, lo)
    cnt_ref[0:1, :] = c_lo
    cnt_ref[1:2, :] = c_hi

    @pl.when(jnp.max(jnp.where(unresolved(c_lo, exact), 1, 0)) > 0)
    def _():
        def bisect(it, carry):
            t, cnt_t = carry
            cand = t | (jnp.int32(1) << (31 - it))
            cand_f = as_threshold(cand)
            c = count(lambda blk, r0: blk >= cand_f)
            ok = c >= ksel
            return jnp.where(ok, cand, t), jnp.where(ok, c, cnt_t)

        thr_key, cnt_ge = lax.fori_loop(0, 32, bisect, (zero, zero))
        t = as_threshold(thr_key)
        t = jnp.maximum(jnp.where(t != t, -F32_MAX, t), -F32_MAX)
        thr_ref[...] = t
        cnt_ref[0:1, :] = cnt_ge
        cnt_ref[1:2, :] = count(lambda blk, r0: blk > t)

    thr = thr_ref[...]
    cnt_ge = cnt_ref[0:1, :]
    need = ksel - cnt_ref[1:2, :]

    jbits = int(seq).bit_length()
    jcut_ref[...] = jnp.full((1, tq), (1 << jbits) - 1, I32)

    @pl.when(jnp.max(jnp.where(cnt_ge > ksel, 1, 0)) > 0)
    def _():
        def jbisect(it, jc):
            cand = jc | (jnp.int32(1) << (jbits - 1 - it))
            f = count(lambda blk, r0: (blk == thr) & ((r0 + row) < cand))
            return jnp.where(f <= need, cand, jc)
        jcut_ref[...] = lax.fori_loop(0, jbits, jbisect, jnp.zeros((1, tq), I32))

    jcut = jcut_ref[...]

    acc_ref[...] = jnp.zeros(acc_ref.shape, F32)

    def selected(kb):
        r0 = pl.multiple_of(kb * tq, tq)
        blk = sc_ref[pl.ds(r0, tq), :]
        return r0, (blk > thr) | ((blk == thr) & ((r0 + row) < jcut))

    qn2 = None
    for j in range(N_KV_HEADS):
        x = qh_ref[j].astype(F32)
        ssq = jnp.sum(x * x, axis=0, keepdims=True)
        for g in range(Q_PER_KV):
            part = ssq[:, tq * g:tq * (g + 1)]
            qn2 = part if qn2 is None else jnp.maximum(qn2, part)
    bound = SHIFT_MARGIN * jnp.sqrt(qn2 * kn_ref[...])
    fixed_shift_ok = jnp.max(bound) <= SHIFT_MAX

    @pl.when(fixed_shift_ok)
    def _():
        def probabilities(kb):
            r0, sel = selected(kb)
            shift = jnp.where(sel, bound, -jnp.inf)
            shift = jnp.concatenate([shift] * Q_PER_KV, axis=1)
            for j in range(N_KV_HEADS):
                s = _dot(k_ref[pl.ds(r0, tq), LANES * (j // 2):LANES * (j // 2 + 1)], qh_ref[j])
                p_ref[j] = jnp.exp2(s + shift).astype(BF16)

        def accumulate(kb):
            for j in range(N_KV_HEADS):
                acc_ref[j] += _dot(vt_ref[kb, j], p_ref[j])

        def attend(kb, _):
            accumulate(kb)
            probabilities(kb + 1)
            return 0

        probabilities(0)
        lax.fori_loop(0, nblk - 1, attend, 0)
        accumulate(nblk - 1)

    @pl.when(jnp.logical_not(fixed_shift_ok))
    def _():
        _online_softmax_attend(k_ref, vt_ref, qh_ref, acc_ref, m_ref, s_ref, mb_ref, selected, nblk, tq)

    slabs = []
    for pair in range(N_HEADS // 2):
        j, g0 = pair // 2, 2 * (pair % 2)
        inv_l = 1.0 / acc_ref[j, HEAD_DIM:HEAD_DIM + 1, tq * g0:tq * (g0 + 2)]
        o2 = acc_ref[j, 0:HEAD_DIM, tq * g0:tq * (g0 + 2)] * inv_l
        slabs.append(jnp.concatenate([o2[:, 0:tq], o2[:, tq:2 * tq]], axis=0).T)
    att = jnp.concatenate(slabs, axis=1)
    y = _dot((att * gate_ref[...].astype(F32)).astype(BF16), wout_ref[...])
    hh = h1_ref[...] + y
    ms = jnp.mean(hh * hh, axis=-1, keepdims=True)
    o_ref[0] = hh * lax.rsqrt(ms + NORM_EPS) * fg_ref[...]


def _online_softmax_attend(k_ref, vt_ref, qh_ref, acc_ref, m_ref, s_ref, mb_ref, selected, nblk, tq):
    m_ref[...] = jnp.full(m_ref.shape, NEG_BIG, F32)

    def logits(kb, slot):
        r0, sel = selected(kb)
        bias = jnp.where(sel, 0.0, -jnp.inf)
        bias = jnp.concatenate([bias] * Q_PER_KV, axis=1)
        for j in range(N_KV_HEADS):
            s = _dot(k_ref[pl.ds(r0, tq), LANES * (j // 2):LANES * (j // 2 + 1)], qh_ref[j]) + bias
            s_ref[slot, j] = s
            mb_ref[slot, j:j + 1, :] = jnp.max(s, axis=0, keepdims=True)

    def accumulate(kb, slot):
        for j in range(N_KV_HEADS):
            m_old = m_ref[j:j + 1, :]
            m_new = jnp.maximum(m_old, mb_ref[slot, j:j + 1, :])
            m_ref[j:j + 1, :] = m_new
            p = jnp.exp2(s_ref[slot, j] - m_new).astype(BF16)
            acc_ref[j] = acc_ref[j] * jnp.exp2(m_old - m_new) + _dot(vt_ref[kb, j], p)

    def attend_pair(t, _):
        kb = 2 * t
        logits(kb + 1, 1)
        accumulate(kb, 0)
        logits(kb + 2, 0)
        accumulate(kb + 1, 1)
        return 0

    logits(0, 0)
    npair = (nblk - 1) // 2
    lax.fori_loop(0, npair, attend_pair, 0)
    done = 2 * npair

    @pl.when(nblk - done == 1)
    def _():
        accumulate(done, 0)

    @pl.when(nblk - done == 2)
    def _():
        logits(done + 1, 1)
        accumulate(done, 0)
        accumulate(done + 1, 1)


def _attention(q, qi, wi, gate, h1, k, v, ki, w_out, final_g, *, seq, ksel):
    tq = ATT_TILE
    ntile = seq // tq
    tile = lambda w_: pl.BlockSpec((tq, w_), lambda b, i: (b * ntile + i, 0))
    whole = lambda w_: pl.BlockSpec((seq, w_), lambda b, i: (b, 0))
    const = lambda shape: pl.BlockSpec(shape, lambda b, i: (0,) * len(shape))
    return pl.pallas_call(
        functools.partial(_attn_kernel, seq=seq, ksel=ksel),
        grid=(NBATCH, seq // tq),
        in_specs=[tile(ATT_WIDTH), tile(IDX_HEADS * IDX_DIM), tile(LANES), tile(ATT_WIDTH), tile(D_MODEL),
                  whole(KV_WIDTH), whole(KV_WIDTH), whole(LANES),
                  const((ATT_WIDTH, D_MODEL)), const((1, D_MODEL))],
        out_specs=pl.BlockSpec((1, tq, D_MODEL), lambda b, i: (b, i, 0)),
        out_shape=jax.ShapeDtypeStruct((NBATCH, seq, D_MODEL), F32),
        scratch_shapes=[
            pltpu.VMEM((seq // tq, N_KV_HEADS, PV_ROWS, tq), BF16),
            pltpu.VMEM((seq, tq), F32),
            pltpu.VMEM((N_KV_HEADS, LANES, Q_PER_KV * tq), BF16),
            pltpu.VMEM((LANES, IDX_HEADS * tq), BF16),
            pltpu.VMEM((N_KV_HEADS, PV_ROWS, Q_PER_KV * tq), F32),
            pltpu.VMEM((N_KV_HEADS, Q_PER_KV * tq), F32),
            pltpu.VMEM((1, tq), I32),
            pltpu.VMEM((2, N_KV_HEADS, tq, Q_PER_KV * tq), F32),
            pltpu.VMEM((2, N_KV_HEADS, Q_PER_KV * tq), F32),
            pltpu.VMEM((1, tq), F32),
            pltpu.VMEM((2, tq), I32),
            pltpu.VMEM((1, tq), F32),
            pltpu.VMEM((N_KV_HEADS, tq, Q_PER_KV * tq), BF16),
        ],
        compiler_params=pltpu.CompilerParams(dimension_semantics=("arbitrary", "arbitrary"),
                                             vmem_limit_bytes=VMEM_LIMIT),
        name="dsa_attention",
    )(q, qi, wi, gate, h1, k, v, ki, w_out, final_g)


def _rope_tables(seq):
    pos = jnp.arange(seq, dtype=F32)
    inv = ROPE_THETA ** (-jnp.arange(0, ROT_DIM, 2, dtype=F32) / ROT_DIM)
    ang = pos[:, None] * inv[None, :]
    cos, sin = jnp.cos(ang), jnp.sin(ang)
    ones = jnp.ones((seq, HEAD_DIM - ROT_DIM), F32)
    return (jnp.concatenate([cos, cos, ones, cos, cos, ones], axis=1),
            jnp.concatenate([-sin, sin, 0.0 * ones, -sin, sin, 0.0 * ones], axis=1))


def _reorder_odd_weights(w):
    o = np.cumsum([0, ATT_WIDTH, ATT_WIDTH, KV_WIDTH, KV_WIDTH, IDX_HEADS * IDX_DIM, IDX_DIM, IDX_HEADS]).tolist()
    q, z, k, v, qi, ki, wi = [w[:, o[j]:o[j + 1]] for j in range(7)]
    pad = jnp.zeros((w.shape[0], LANES - IDX_DIM - IDX_HEADS), w.dtype)
    return jnp.concatenate([q, k, qi, ki, wi, pad, z, v], axis=1)


def kernel(x, norm_g, e_w_in, e_lam_re, e_lam_im, e_log_step, e_b_re, e_b_im, e_c_re, e_c_im, e_d_skip,
           e_w_glu, e_b_glu, e_conv_w, e_conv_b, e_ln_g, e_ln_b, e_w_out, o_w_in, o_w_out, final_g):
    bsz, seq, dm = x.shape
    assert bsz == NBATCH and dm == D_MODEL and seq % (2 * ATT_TILE) == 0
    assert norm_g.shape[0] == 2 and e_w_in.shape[0] == 1 and o_w_in.shape[0] == 1
    tt = 64
    ksel = min(TOPK_MAX, seq // 4)
    row = lambda a: a.reshape(1, -1).astype(F32)

    ab_re, ab_im, bb_re, bb_im = _s5_prep(e_lam_re[0], e_lam_im[0], e_log_step[0], e_b_re[0], e_b_im[0])
    bc, cc = _s5_block_weights(bb_re, bb_im, e_c_re[0], e_c_im[0])
    a_re = jnp.broadcast_to(ab_re.reshape(1, S5_NSTATE), (NBATCH, S5_NSTATE))
    a_im = jnp.broadcast_to(ab_im.reshape(1, S5_NSTATE), (NBATCH, S5_NSTATE))
    conv_w = jnp.broadcast_to(e_conv_w[0][:, None, :], (CONV_KERNEL, NBATCH, CONV_WIDTH))

    h1 = _layer0(x.reshape(NBATCH, seq * D_MODEL), row(norm_g[0]), e_w_in[0].astype(BF16), a_re, a_im, bc, cc,
                 row(e_d_skip[0]), e_w_glu[0].astype(BF16), row(e_b_glu[0]), conv_w, row(e_conv_b[0]),
                 row(e_ln_g[0]), row(e_ln_b[0]), e_w_out[0].astype(BF16), seq=seq, tt=tt)
    h1 = h1.reshape(NBATCH * seq, D_MODEL)

    cos_t, sin_t = _rope_tables(seq)
    q, k, qi, ki, wi, gate, v = _inproj1(h1, row(norm_g[1]), _reorder_odd_weights(o_w_in[0]).astype(BF16),
                                         cos_t, sin_t, seq=seq, rows=512)
    return _attention(q, qi, wi, gate, h1, k, v, ki, o_w_out[0].astype(BF16), row(final_g), seq=seq, ksel=ksel)
```

```python
import functools
import math

import numpy as np
import jax
import jax.numpy as jnp
from jax import lax
from jax.experimental import pallas as pl
from jax.experimental.pallas import tpu as pltpu

F32 = jnp.float32
BF16 = jnp.bfloat16
I32 = jnp.int32

NBATCH = 8
D_MODEL = 1024
CHUNK = 64
NORM_EPS = 1e-6
S5_WIDTH = 512
S5_GROUP = 16
S5_GROUPS = 32
S5_STATE = 64
S5_NSTATE = S5_GROUPS * S5_STATE
S5_SLABS = 4
CONV_WIDTH = 512
CONV_KERNEL = 31
CONV_HALO = (CONV_KERNEL - 1) * NBATCH
EVEN_IN = 2 * S5_WIDTH + 3 * CONV_WIDTH
N_HEADS = 16
HEAD_DIM = 64
N_KV_HEADS = 4
Q_PER_KV = 4
ATT_WIDTH = 1024
KV_WIDTH = 256
IDX_HEADS = 8
IDX_DIM = 64
TOPK_MAX = 256
ROPE_THETA = 500000.0
ROT_DIM = 16
LANES = 128
ODD_PAD = 3200
ATT_TILE = 256
PV_ROWS = HEAD_DIM + 16
BISECT_STEPS = 16
PEEL_STEPS = 6
SHIFT_MARGIN = 1.02
SHIFT_MAX = 50.0
INT_MIN = np.int32(-2 ** 31)
F32_MAX = float(np.finfo(np.float32).max)
NEG_BIG = -1e38
LOG2E = math.log2(math.e)

VMEM_LIMIT = 56 * 1024 * 1024


def _dot(a, b):
    return jnp.dot(a, b, preferred_element_type=F32)


def _sigmoid(x):
    return 1.0 / (1.0 + jnp.exp(-x))


def _silu(x):
    return x * _sigmoid(x)


def _gelu_tanh(x):
    c = math.sqrt(2.0 / math.pi)
    return 0.5 * x * (1.0 + jnp.tanh(c * (x + 0.044715 * (x * x * x))))


def _s5_prep_kernel(lr_ref, li_ref, ls_ref, br_ref, bi_ref, abr_ref, abi_ref, bbr_ref, bbi_ref):
    lr = jnp.minimum(lr_ref[...], -1e-4)
    li = li_ref[...]
    dt = jnp.exp(ls_ref[...])
    mag = jnp.exp(lr * dt)
    ab_re = mag * jnp.cos(li * dt)
    ab_im = mag * jnp.sin(li * dt)
    den = lr * lr + li * li
    nr = ab_re - 1.0
    ni = ab_im
    k_re = (nr * lr + ni * li) / den
    k_im = (ni * lr - nr * li) / den
    br = br_ref[...]
    bi = bi_ref[...]
    abr_ref[...] = ab_re
    abi_ref[...] = ab_im
    bbr_ref[...] = k_re * br - k_im * bi
    bbi_ref[...] = k_re * bi + k_im * br


def _s5_prep(lam_re, lam_im, log_step, b_re, b_im):
    g, n, c = b_re.shape
    shape2 = (g * n * c // LANES, LANES)
    expand = lambda a: jnp.broadcast_to(a[:, :, None], (g, n, c)).reshape(shape2)
    ls = jnp.broadcast_to(log_step[:, None], (g, n))
    outs = pl.pallas_call(
        _s5_prep_kernel,
        out_shape=[jax.ShapeDtypeStruct(shape2, F32)] * 4,
        name="s5_prep",
    )(expand(lam_re), expand(lam_im), expand(ls), b_re.reshape(shape2), b_im.reshape(shape2))
    ab_re, ab_im, bb_re, bb_im = [o.reshape(g, n, c) for o in outs]
    return ab_re[:, :, 0], ab_im[:, :, 0], bb_re, bb_im


def _s5_block_weights(bb_re, bb_im, c_re, c_im):
    gl = S5_GROUPS // S5_SLABS
    eye = jnp.eye(gl, dtype=F32)

    def in_map(bb):
        t = bb.reshape(S5_SLABS, gl, S5_STATE, S5_GROUP).transpose(0, 1, 3, 2)
        t = t[:, :, :, None, :] * eye[None, :, None, :, None]
        return t.reshape(S5_SLABS, gl * S5_GROUP, gl * S5_STATE)

    def out_map(cc):
        t = cc.reshape(S5_SLABS, gl, S5_GROUP, S5_STATE).transpose(0, 1, 3, 2)
        t = t[:, :, :, None, :] * eye[None, :, None, :, None]
        return t.reshape(S5_SLABS, gl * S5_STATE, gl * S5_GROUP)

    bc = jnp.concatenate([in_map(bb_re), in_map(bb_im)], axis=2).astype(BF16)
    cc = jnp.concatenate([out_map(c_re), out_map(-c_im)], axis=1).astype(BF16)
    return bc, cc


def _layer0_kernel(x_ref, g_ref, win_ref, are_ref, aim_ref, bc_ref, cc_ref, dsk_ref, wglu_ref, bglu_ref,
                   cw_ref, cb_ref, lng_ref, lnb_ref, wout_ref, o_ref,
                   hn_ref, proj_ref, xs_ref, st_ref, hc_ref, gl_ref, ycat_ref, tm_ref, *, tt):
    rows = NBATCH * tt
    half = S5_NSTATE
    slab_w = half // S5_SLABS

    @pl.when(pl.program_id(0) == 0)
    def _():
        st_ref[...] = jnp.zeros_like(st_ref)
        hc_ref[0:CONV_HALO, :] = jnp.zeros((CONV_HALO, CONV_WIDTH), F32)

    gain = g_ref[...]
    for tp in range(tt // 2):
        xa = x_ref[:, (2 * tp) * D_MODEL:(2 * tp + 1) * D_MODEL]
        xb = x_ref[:, (2 * tp + 1) * D_MODEL:(2 * tp + 2) * D_MODEL]
        xx = jnp.concatenate([xa, xb], axis=0)
        ms = jnp.mean(xx * xx, axis=-1, keepdims=True)
        hn_ref[16 * tp:16 * tp + 16, :] = (xx * lax.rsqrt(ms + NORM_EPS) * gain).astype(BF16)

    proj_ref[...] = _dot(hn_ref[...], win_ref[...])

    for m in range(S5_SLABS):
        u_m = proj_ref[:, LANES * m:LANES * (m + 1)].astype(BF16)
        xm = _dot(u_m, bc_ref[m])
        xs_ref[:, slab_w * m:slab_w * (m + 1)] = xm[:, :slab_w]
        xs_ref[:, half + slab_w * m:half + slab_w * (m + 1)] = xm[:, slab_w:]

    cw = 512
    for cc_i in range(half // cw):
        lo = cc_i * cw
        a_re = are_ref[:, lo:lo + cw]
        a_im = aim_ref[:, lo:lo + cw]

        def step(t, carry, lo=lo, a_re=a_re, a_im=a_im):
            s_re, s_im = carry
            r0 = pl.multiple_of(t * NBATCH, NBATCH)
            x_re = xs_ref[pl.ds(r0, NBATCH), lo:lo + cw]
            x_im = xs_ref[pl.ds(r0, NBATCH), half + lo:half + lo + cw]
            n_re = a_re * s_re - a_im * s_im + x_re
            n_im = a_re * s_im + a_im * s_re + x_im
            xs_ref[pl.ds(r0, NBATCH), lo:lo + cw] = n_re
            xs_ref[pl.ds(r0, NBATCH), half + lo:half + lo + cw] = n_im
            return n_re, n_im

        s_re, s_im = lax.fori_loop(0, tt, step, (st_ref[:, lo:lo + cw], st_ref[:, half + lo:half + lo + cw]),
                                   unroll=True)
        st_ref[:, lo:lo + cw] = s_re
        st_ref[:, half + lo:half + lo + cw] = s_im

    for m in range(S5_SLABS):
        s_re = xs_ref[:, slab_w * m:slab_w * (m + 1)].astype(BF16)
        s_im = xs_ref[:, half + slab_w * m:half + slab_w * (m + 1)].astype(BF16)
        y_m = _dot(s_re, cc_ref[m, 0:slab_w, :]) + _dot(s_im, cc_ref[m, slab_w:2 * slab_w, :])
        y_m = y_m + proj_ref[:, LANES * m:LANES * (m + 1)] * dsk_ref[:, LANES * m:LANES * (m + 1)]
        gl_ref[:, LANES * m:LANES * (m + 1)] = _gelu_tanh(y_m)
    gl = gl_ref[...]
    out_a = gl * _sigmoid(_dot(gl.astype(BF16), wglu_ref[...]) + bglu_ref[...])
    ycat_ref[:, 0:S5_WIDTH] = (out_a * _silu(proj_ref[:, S5_WIDTH:2 * S5_WIDTH])).astype(BF16)

    o = 2 * S5_WIDTH
    hc_ref[CONV_HALO:CONV_HALO + rows, :] = (
        proj_ref[:, o:o + CONV_WIDTH] * _sigmoid(proj_ref[:, o + CONV_WIDTH:o + 2 * CONV_WIDTH]))
    rc = 64
    for cl in range(CONV_WIDTH // LANES):
        lanes = slice(LANES * cl, LANES * (cl + 1))
        taps = [cw_ref[k, :, lanes] for k in range(CONV_KERNEL)]

        def conv_chunk(c, _, lanes=lanes, taps=taps):
            r0 = pl.multiple_of(c * rc, rc)
            accs = [None] * (rc // NBATCH)
            for t in range(rc // NBATCH + CONV_KERNEL - 1):
                h = hc_ref[pl.ds(r0 + NBATCH * t, NBATCH), lanes]
                for rr in range(rc // NBATCH):
                    k = t - rr
                    if 0 <= k < CONV_KERNEL:
                        accs[rr] = h * taps[k] if accs[rr] is None else accs[rr] + h * taps[k]
            gl_ref[pl.ds(r0, rc), lanes] = jnp.concatenate(accs, axis=0)
            return 0

        lax.fori_loop(0, rows // rc, conv_chunk, 0)
    hc_ref[0:CONV_HALO, :] = hc_ref[rows:rows + CONV_HALO, :]
    h = gl_ref[...] + cb_ref[...]
    mu = jnp.mean(h, axis=-1, keepdims=True)
    var = jnp.mean(jnp.square(h - mu), axis=-1, keepdims=True)
    hf = (h - mu) * lax.rsqrt(var + NORM_EPS) * lng_ref[...] + lnb_ref[...]
    z_b = proj_ref[:, o + 2 * CONV_WIDTH:o + 3 * CONV_WIDTH]
    ycat_ref[:, S5_WIDTH:S5_WIDTH + CONV_WIDTH] = (_silu(hf) * _silu(z_b)).astype(BF16)

    y = _dot(ycat_ref[...], wout_ref[...])
    for t in range(tt):
        h = y[NBATCH * t:NBATCH * (t + 1), :] + x_ref[:, t * D_MODEL:(t + 1) * D_MODEL]
        for c in range(D_MODEL // LANES):
            tm_ref[c, NBATCH * t:NBATCH * (t + 1), :] = h[:, LANES * c:LANES * (c + 1)]
    for b in range(NBATCH):
        for c in range(D_MODEL // LANES):
            o_ref[b, :, LANES * c:LANES * (c + 1)] = tm_ref[c, pl.ds(b, tt, stride=NBATCH), :]


def _layer0(x2, gain, w_in, a_re, a_im, bc, cc, d_skip, w_glu, b_glu, conv_w, conv_b, ln_g, ln_b, w_out, *, seq, tt):
    rows = NBATCH * tt
    const = lambda shape: pl.BlockSpec(shape, lambda i: (0,) * len(shape))
    return pl.pallas_call(
        functools.partial(_layer0_kernel, tt=tt),
        grid=(seq // tt,),
        in_specs=[
            pl.BlockSpec((NBATCH, tt * D_MODEL), lambda i: (0, i)),
            const((1, D_MODEL)),
            const((D_MODEL, EVEN_IN)),
            const((NBATCH, S5_NSTATE)),
            const((NBATCH, S5_NSTATE)),
            const((S5_SLABS, LANES, 2 * S5_NSTATE // S5_SLABS)),
            const((S5_SLABS, 2 * S5_NSTATE // S5_SLABS, LANES)),
            const((1, S5_WIDTH)),
            const((S5_WIDTH, S5_WIDTH)),
            const((1, S5_WIDTH)),
            const((CONV_KERNEL, NBATCH, CONV_WIDTH)),
            const((1, CONV_WIDTH)),
            const((1, CONV_WIDTH)),
            const((1, CONV_WIDTH)),
            const((S5_WIDTH + CONV_WIDTH, D_MODEL)),
        ],
        out_specs=pl.BlockSpec((NBATCH, tt, D_MODEL), lambda i: (0, i, 0)),
        out_shape=jax.ShapeDtypeStruct((NBATCH, seq, D_MODEL), F32),
        scratch_shapes=[
            pltpu.VMEM((rows, D_MODEL), BF16),
            pltpu.VMEM((rows, EVEN_IN), F32),
            pltpu.VMEM((rows, 2 * S5_NSTATE), F32),
            pltpu.VMEM((NBATCH, 2 * S5_NSTATE), F32),
            pltpu.VMEM((CONV_HALO + rows, CONV_WIDTH), F32),
            pltpu.VMEM((rows, S5_WIDTH), F32),
            pltpu.VMEM((rows, S5_WIDTH + CONV_WIDTH), BF16),
            pltpu.VMEM((D_MODEL // LANES, rows, LANES), F32),
        ],
        compiler_params=pltpu.CompilerParams(dimension_semantics=("arbitrary",), vmem_limit_bytes=VMEM_LIMIT),
        name="layer0",
    )(x2, gain, w_in, a_re, a_im, bc, cc, d_skip, w_glu, b_glu, conv_w, conv_b, ln_g, ln_b, w_out)


def _inproj1_kernel(h_ref, g_ref, w_ref, cos_ref, sin_ref,
                    q_ref, k_ref, qi_ref, ki_ref, wi_ref, gate_ref, v_ref):
    h = h_ref[...]
    ms = jnp.mean(h * h, axis=-1, keepdims=True)
    hn = (h * lax.rsqrt(ms + NORM_EPS) * g_ref[...]).astype(BF16)
    proj = _dot(hn, w_ref[...])
    cos_t = cos_ref[...]
    sin_t = sin_ref[...]
    lane = lax.broadcasted_iota(I32, cos_t.shape, 1)
    first = (lane & (HEAD_DIM - 1)) < (ROT_DIM // 2)
    low = lane < HEAD_DIM

    def slab(c):
        return proj[:, LANES * c:LANES * (c + 1)]

    def rope(xs, ct, st):
        partner = jnp.where(first, pltpu.roll(xs, LANES - ROT_DIM // 2, 1), pltpu.roll(xs, ROT_DIM // 2, 1))
        return xs * ct + partner * st

    q_scale = (HEAD_DIM ** -0.5) * LOG2E
    for c in range(8):
        q_ref[:, LANES * c:LANES * (c + 1)] = (rope(slab(c), cos_t, sin_t) * q_scale).astype(BF16)
    for c in range(2):
        k_ref[:, LANES * c:LANES * (c + 1)] = rope(slab(8 + c), cos_t, sin_t).astype(BF16)
    for c in range(4):
        qi_ref[:, LANES * c:LANES * (c + 1)] = rope(slab(10 + c), cos_t, sin_t).astype(BF16)
    kiw = rope(slab(14), jnp.where(low, cos_t, 1.0), jnp.where(low, sin_t, 0.0))
    ki_ref[...] = kiw.astype(BF16)
    wi_ref[...] = kiw
    for c in range(8):
        gate_ref[:, LANES * c:LANES * (c + 1)] = _silu(slab(15 + c)).astype(BF16)
    for c in range(2):
        v_ref[:, LANES * c:LANES * (c + 1)] = slab(23 + c).astype(BF16)


def _inproj1(h1, gain, w, cos_t, sin_t, *, seq, rows):
    n = seq * NBATCH
    const = lambda shape: pl.BlockSpec(shape, lambda i: (0,) * len(shape))
    rb = lambda w_: pl.BlockSpec((rows, w_), lambda i: (i, 0))
    pos = pl.BlockSpec((rows, LANES), lambda i: (i % (seq // rows), 0))
    return pl.pallas_call(
        _inproj1_kernel,
        grid=(n // rows,),
        in_specs=[rb(D_MODEL), const((1, D_MODEL)), const((D_MODEL, ODD_PAD)), pos, pos],
        out_specs=[rb(ATT_WIDTH), rb(KV_WIDTH), rb(IDX_HEADS * IDX_DIM), rb(LANES), rb(LANES), rb(ATT_WIDTH),
                   rb(KV_WIDTH)],
        out_shape=[
            jax.ShapeDtypeStruct((n, ATT_WIDTH), BF16),
            jax.ShapeDtypeStruct((n, KV_WIDTH), BF16),
            jax.ShapeDtypeStruct((n, IDX_HEADS * IDX_DIM), BF16),
            jax.ShapeDtypeStruct((n, LANES), BF16),
            jax.ShapeDtypeStruct((n, LANES), F32),
            jax.ShapeDtypeStruct((n, ATT_WIDTH), BF16),
            jax.ShapeDtypeStruct((n, KV_WIDTH), BF16),
        ],
        compiler_params=pltpu.CompilerParams(dimension_semantics=("arbitrary",), vmem_limit_bytes=VMEM_LIMIT),
        name="inproj1",
    )(h1, gain, w, cos_t, sin_t)


def _attn_kernel(q_ref, qi_ref, wi_ref, gate_ref, h1_ref, k_ref, v_ref, ki_ref, wout_ref, fg_ref, o_ref,
                 vt_ref, sc_ref, qh_ref, qih_ref, acc_ref, m_ref, jcut_ref, s_ref, mb_ref, thr_ref, cnt_ref,
                 kn_ref, p_ref, *, seq, ksel):
    tq = ATT_TILE
    i = pl.program_id(1)
    nblk = i + 1

    @pl.when(i == 0)
    def _():
        def transpose_v(kb, _):
            r0 = pl.multiple_of(kb * tq, tq)
            vt = v_ref[pl.ds(r0, tq), :].astype(F32).T
            for j in range(N_KV_HEADS):
                vt_ref[kb, j, 0:HEAD_DIM, :] = vt[HEAD_DIM * j:HEAD_DIM * (j + 1), :].astype(BF16)
                vt_ref[kb, j, HEAD_DIM:PV_ROWS, :] = jnp.ones((PV_ROWS - HEAD_DIM, tq), BF16)
            return 0
        lax.fori_loop(0, seq // tq, transpose_v, 0)

        def key_norm(kb, mx):
            r0 = pl.multiple_of(kb * tq, tq)
            kt = k_ref[pl.ds(r0, tq), :].astype(F32).T
            sq = kt * kt
            for j in range(N_KV_HEADS):
                mx = jnp.maximum(mx, jnp.sum(sq[HEAD_DIM * j:HEAD_DIM * (j + 1), :], axis=0, keepdims=True))
            return mx
        kmax = lax.fori_loop(0, seq // tq, key_norm, jnp.zeros((1, tq), F32))
        kn_ref[...] = jnp.zeros((1, tq), F32) + jnp.max(kmax)

    zero_half = jnp.zeros((HEAD_DIM, tq), BF16)
    for c in range(ATT_WIDTH // LANES):
        t = q_ref[:, LANES * c:LANES * (c + 1)].astype(F32).T
        for half in range(2):
            j, g = divmod(2 * c + half, Q_PER_KV)
            lo = (j % 2) * HEAD_DIM
            cols = slice(tq * g, tq * (g + 1))
            qh_ref[j, lo:lo + HEAD_DIM, cols] = t[HEAD_DIM * half:HEAD_DIM * (half + 1), :].astype(BF16)
            qh_ref[j, HEAD_DIM - lo:2 * HEAD_DIM - lo, cols] = zero_half
    for c in range(IDX_HEADS * IDX_DIM // LANES):
        t = qi_ref[:, LANES * c:LANES * (c + 1)].astype(F32).T
        for half in range(2):
            h = 2 * c + half
            qih_ref[0:IDX_DIM, tq * h:tq * (h + 1)] = t[IDX_DIM * half:IDX_DIM * (half + 1), :].astype(BF16)
    qih_ref[IDX_DIM:LANES, :] = jnp.zeros((LANES - IDX_DIM, IDX_HEADS * tq), BF16)
    w8 = wi_ref[...].T[IDX_DIM:IDX_DIM + IDX_HEADS, :] * (IDX_HEADS ** -0.5) * (IDX_DIM ** -0.5)

    row = lax.broadcasted_iota(I32, (tq, tq), 0)
    col = lax.broadcasted_iota(I32, (tq, tq), 1)
    row_chunk = row >> 6
    col_chunk = col >> 6

    def tree(op, x):
        groups = [x[8 * r:8 * (r + 1)] for r in range(x.shape[0] // 8)]
        chains = groups[:4]
        for r in range(4, len(groups)):
            chains[r % 4] = op(chains[r % 4], groups[r])
        return op(op(chains[0], chains[1]), op(chains[2], chains[3]))

    half_heads = IDX_HEADS // 2

    def index_logits(kb):
        ki = ki_ref[pl.ds(pl.multiple_of(kb * tq, tq), tq), :]
        for part in range(2):
            s_ref[0, part] = _dot(ki, qih_ref[:, part * half_heads * tq:(part + 1) * half_heads * tq])

    def score_block(kb, carry):
        smax, smin = carry
        r0 = pl.multiple_of(kb * tq, tq)
        acc = None
        for h in range(IDX_HEADS):
            sh = s_ref[0, h // half_heads, :, tq * (h % half_heads):tq * (h % half_heads + 1)]
            term = jnp.maximum(sh, 0.0) * w8[h:h + 1, :]
            acc = term if acc is None else acc + term
        acc = jnp.where(acc == 0.0, 0.0, acc)
        later = ((kb - i) * (tq // CHUNK) + row_chunk) > col_chunk
        sc_ref[pl.ds(r0, tq), :] = jnp.where(later, -jnp.inf, acc)
        smax = jnp.maximum(smax, tree(jnp.maximum, jnp.where(later, -jnp.inf, acc)))
        smin = jnp.minimum(smin, tree(jnp.minimum, acc))
        return smax, smin

    def score_step(kb, carry):
        carry = score_block(kb, carry)
        index_logits(kb + 1)
        return carry

    index_logits(0)
    carry = lax.fori_loop(0, nblk - 1, score_step,
                          (jnp.full((8, tq), -jnp.inf, F32), jnp.full((8, tq), jnp.inf, F32)))
    smax, smin = score_block(nblk - 1, carry)
    smax = jnp.max(smax, axis=0, keepdims=True)
    smin = jnp.min(smin, axis=0, keepdims=True)

    @pl.when(nblk % 2 == 1)
    def _():
        sc_ref[pl.ds(pl.multiple_of(nblk * tq, tq), tq), :] = jnp.full((tq, tq), -jnp.inf, F32)

    def lane_pass(elem, op, init):
        def body(kb2, acc):
            r0 = pl.multiple_of(kb2 * 2 * tq, 2 * tq)
            acc = op(acc, tree(op, elem(sc_ref[pl.ds(r0, tq), :], r0)))
            return op(acc, tree(op, elem(sc_ref[pl.ds(r0 + tq, tq), :], r0 + tq)))
        return lax.fori_loop(0, (nblk + 1) // 2, body, init)

    def count(pred):
        acc = lane_pass(lambda blk, r0: jnp.where(pred(blk, r0), 1, 0).astype(I32), jnp.add,
                        jnp.zeros((8, tq), I32))
        return jnp.sum(acc, axis=0, keepdims=True)

    def max_below(bound):
        acc = lane_pass(lambda blk, r0: jnp.where(blk < bound, blk, -jnp.inf), jnp.maximum,
                        jnp.full((8, tq), -jnp.inf, F32))
        return jnp.max(acc, axis=0, keepdims=True)

    def as_threshold(key):
        ks = key ^ INT_MIN
        return pltpu.bitcast(jnp.where(ks >= 0, ks, ks ^ np.int32(0x7FFFFFFF)), F32)

    def next_up(x):
        bits = pltpu.bitcast(x, I32)
        ks = jnp.where(bits >= 0, bits, bits ^ np.int32(0x7FFFFFFF)) + 1
        return pltpu.bitcast(jnp.where(ks >= 0, ks, ks ^ np.int32(0x7FFFFFFF)), F32)

    n_adm = (CHUNK * (tq // CHUNK) * i + CHUNK) + CHUNK * col_chunk[0:1, :]
    few = n_adm <= ksel

    def halve(_, st):
        lo, hi, c_lo, c_hi = st
        t = 0.5 * lo + 0.5 * hi
        inside = (t > lo) & (t < hi)
        t = jnp.where(inside, t, lo)
        c = count(lambda blk, r0: blk >= t)
        up = inside & (c >= ksel)
        dn = inside & (c < ksel)
        return jnp.where(up, t, lo), jnp.where(dn, t, hi), jnp.where(up, c, c_lo), jnp.where(dn, c, c_hi)

    zero = jnp.zeros((1, tq), I32)
    lo, hi, c_lo, c_hi = lax.fori_loop(
        0, BISECT_STEPS, halve, (smin, next_up(jnp.minimum(smax, F32_MAX)), n_adm, zero))

    def unresolved(c_lo, exact):
        return (c_lo > ksel) & (exact == 0) & jnp.logical_not(few)

    def peel_cond(st):
        it, lo, hi, c_lo, c_hi, exact = st
        return (it < PEEL_STEPS) & (jnp.max(jnp.where(unresolved(c_lo, exact), 1, 0)) > 0)

    def peel(st):
        it, lo, hi, c_lo, c_hi, exact = st
        v = max_below(hi)
        c_v = count(lambda blk, r0: blk >= v)
        act = unresolved(c_lo, exact)
        hit = act & (c_v >= ksel)
        miss = act & (c_v < ksel)
        return (it + 1, jnp.where(hit, v, lo), jnp.where(miss, v, hi), jnp.where(hit, c_v, c_lo),
                jnp.where(miss, c_v, c_hi), jnp.where(hit, 1, exact))

    _, lo, hi, c_lo, c_hi, exact = lax.while_loop(peel_cond, peel, (jnp.int32(0), lo, hi, c_lo, c_hi, zero))
    thr_ref[...] = jnp.where(few, -F32_MAX, lo)
    cnt_ref[0:1, :] = c_lo
    cnt_ref[1:2, :] = c_hi

    @pl.when(jnp.max(jnp.where(unresolved(c_lo, exact), 1, 0)) > 0)
    def _():
        def bisect(it, carry):
            t, cnt_t = carry
            cand = t | (jnp.int32(1) << (31 - it))
            cand_f = as_threshold(cand)
            c = count(lambda blk, r0: blk >= cand_f)
            ok = c >= ksel
            return jnp.where(ok, cand, t), jnp.where(ok, c, cnt_t)

        thr_key, cnt_ge = lax.fori_loop(0, 32, bisect, (zero, zero))
        t = as_threshold(thr_key)
        t = jnp.maximum(jnp.where(t != t, -F32_MAX, t), -F32_MAX)
        thr_ref[...] = t
        cnt_ref[0:1, :] = cnt_ge
        cnt_ref[1:2, :] = count(lambda blk, r0: blk > t)

    thr = thr_ref[...]
    cnt_ge = cnt_ref[0:1, :]
    need = ksel - cnt_ref[1:2, :]

    jbits = int(seq).bit_length()
    jcut_ref[...] = jnp.full((1, tq), (1 << jbits) - 1, I32)

    @pl.when(jnp.max(jnp.where(cnt_ge > ksel, 1, 0)) > 0)
    def _():
        def jbisect(it, jc):
            cand = jc | (jnp.int32(1) << (jbits - 1 - it))
            f = count(lambda blk, r0: (blk == thr) & ((r0 + row) < cand))
            return jnp.where(f <= need, cand, jc)
        jcut_ref[...] = lax.fori_loop(0, jbits, jbisect, jnp.zeros((1, tq), I32))

    jcut = jcut_ref[...]

    acc_ref[...] = jnp.zeros(acc_ref.shape, F32)

    def selected(kb):
        r0 = pl.multiple_of(kb * tq, tq)
        blk = sc_ref[pl.ds(r0, tq), :]
        return r0, (blk > thr) | ((blk == thr) & ((r0 + row) < jcut))

    qn2 = None
    for j in range(N_KV_HEADS):
        x = qh_ref[j].astype(F32)
        ssq = jnp.sum(x * x, axis=0, keepdims=True)
        for g in range(Q_PER_KV):
            part = ssq[:, tq * g:tq * (g + 1)]
            qn2 = part if qn2 is None else jnp.maximum(qn2, part)
    bound = SHIFT_MARGIN * jnp.sqrt(qn2 * kn_ref[...])
    fixed_shift_ok = jnp.max(bound) <= SHIFT_MAX

    @pl.when(fixed_shift_ok)
    def _():
        def probabilities(kb):
            r0, sel = selected(kb)
            shift = jnp.where(sel, bound, -jnp.inf)
            shift = jnp.concatenate([shift] * Q_PER_KV, axis=1)
            for j in range(N_KV_HEADS):
                s = _dot(k_ref[pl.ds(r0, tq), LANES * (j // 2):LANES * (j // 2 + 1)], qh_ref[j])
                p_ref[j] = jnp.exp2(s + shift).astype(BF16)

        def accumulate(kb):
            for j in range(N_KV_HEADS):
                acc_ref[j] += _dot(vt_ref[kb, j], p_ref[j])

        def attend(kb, _):
            accumulate(kb)
            probabilities(kb + 1)
            return 0

        probabilities(0)
        lax.fori_loop(0, nblk - 1, attend, 0)
        accumulate(nblk - 1)

    @pl.when(jnp.logical_not(fixed_shift_ok))
    def _():
        _online_softmax_attend(k_ref, vt_ref, qh_ref, acc_ref, m_ref, s_ref, mb_ref, selected, nblk, tq)

    slabs = []
    for pair in range(N_HEADS // 2):
        j, g0 = pair // 2, 2 * (pair % 2)
        inv_l = 1.0 / acc_ref[j, HEAD_DIM:HEAD_DIM + 1, tq * g0:tq * (g0 + 2)]
        o2 = acc_ref[j, 0:HEAD_DIM, tq * g0:tq * (g0 + 2)] * inv_l
        slabs.append(jnp.concatenate([o2[:, 0:tq], o2[:, tq:2 * tq]], axis=0).T)
    att = jnp.concatenate(slabs, axis=1)
    y = _dot((att * gate_ref[...].astype(F32)).astype(BF16), wout_ref[...])
    hh = h1_ref[...] + y
    ms = jnp.mean(hh * hh, axis=-1, keepdims=True)
    o_ref[0] = hh * lax.rsqrt(ms + NORM_EPS) * fg_ref[...]


def _online_softmax_attend(k_ref, vt_ref, qh_ref, acc_ref, m_ref, s_ref, mb_ref, selected, nblk, tq):
    m_ref[...] = jnp.full(m_ref.shape, NEG_BIG, F32)

    def logits(kb, slot):
        r0, sel = selected(kb)
        bias = jnp.where(sel, 0.0, -jnp.inf)
        bias = jnp.concatenate([bias] * Q_PER_KV, axis=1)
        for j in range(N_KV_HEADS):
            s = _dot(k_ref[pl.ds(r0, tq), LANES * (j // 2):LANES * (j // 2 + 1)], qh_ref[j]) + bias
            s_ref[slot, j] = s
            mb_ref[slot, j:j + 1, :] = jnp.max(s, axis=0, keepdims=True)

    def accumulate(kb, slot):
        for j in range(N_KV_HEADS):
            m_old = m_ref[j:j + 1, :]
            m_new = jnp.maximum(m_old, mb_ref[slot, j:j + 1, :])
            m_ref[j:j + 1, :] = m_new
            p = jnp.exp2(s_ref[slot, j] - m_new).astype(BF16)
            acc_ref[j] = acc_ref[j] * jnp.exp2(m_old - m_new) + _dot(vt_ref[kb, j], p)

    def attend_pair(t, _):
        kb = 2 * t
        logits(kb + 1, 1)
        accumulate(kb, 0)
        logits(kb + 2, 0)
        accumulate(kb + 1, 1)
        return 0

    logits(0, 0)
    npair = (nblk - 1) // 2
    lax.fori_loop(0, npair, attend_pair, 0)
    done = 2 * npair

    @pl.when(nblk - done == 1)
    def _():
        accumulate(done, 0)

    @pl.when(nblk - done == 2)
    def _():
        logits(done + 1, 1)
        accumulate(done, 0)
        accumulate(done + 1, 1)


def _attention(q, qi, wi, gate, h1, k, v, ki, w_out, final_g, *, seq, ksel):
    tq = ATT_TILE
    ntile = seq // tq
    tile = lambda w_: pl.BlockSpec((tq, w_), lambda b, i: (b * ntile + i, 0))
    whole = lambda w_: pl.BlockSpec((seq, w_), lambda b, i: (b, 0))
    const = lambda shape: pl.BlockSpec(shape, lambda b, i: (0,) * len(shape))
    return pl.pallas_call(
        functools.partial(_attn_kernel, seq=seq, ksel=ksel),
        grid=(NBATCH, seq // tq),
        in_specs=[tile(ATT_WIDTH), tile(IDX_HEADS * IDX_DIM), tile(LANES), tile(ATT_WIDTH), tile(D_MODEL),
                  whole(KV_WIDTH), whole(KV_WIDTH), whole(LANES),
                  const((ATT_WIDTH, D_MODEL)), const((1, D_MODEL))],
        out_specs=pl.BlockSpec((1, tq, D_MODEL), lambda b, i: (b, i, 0)),
        out_shape=jax.ShapeDtypeStruct((NBATCH, seq, D_MODEL), F32),
        scratch_shapes=[
            pltpu.VMEM((seq // tq, N_KV_HEADS, PV_ROWS, tq), BF16),
            pltpu.VMEM((seq, tq), F32),
            pltpu.VMEM((N_KV_HEADS, LANES, Q_PER_KV * tq), BF16),
            pltpu.VMEM((LANES, IDX_HEADS * tq), BF16),
            pltpu.VMEM((N_KV_HEADS, PV_ROWS, Q_PER_KV * tq), F32),
            pltpu.VMEM((N_KV_HEADS, Q_PER_KV * tq), F32),
            pltpu.VMEM((1, tq), I32),
            pltpu.VMEM((2, N_KV_HEADS, tq, Q_PER_KV * tq), F32),
            pltpu.VMEM((2, N_KV_HEADS, Q_PER_KV * tq), F32),
            pltpu.VMEM((1, tq), F32),
            pltpu.VMEM((2, tq), I32),
            pltpu.VMEM((1, tq), F32),
            pltpu.VMEM((N_KV_HEADS, tq, Q_PER_KV * tq), BF16),
        ],
        compiler_params=pltpu.CompilerParams(dimension_semantics=("arbitrary", "arbitrary"),
                                             vmem_limit_bytes=VMEM_LIMIT),
        name="dsa_attention",
    )(q, qi, wi, gate, h1, k, v, ki, w_out, final_g)


def _rope_tables(seq):
    pos = jnp.arange(seq, dtype=F32)
    inv = ROPE_THETA ** (-jnp.arange(0, ROT_DIM, 2, dtype=F32) / ROT_DIM)
    ang = pos[:, None] * inv[None, :]
    cos, sin = jnp.cos(ang), jnp.sin(ang)
    ones = jnp.ones((seq, HEAD_DIM - ROT_DIM), F32)
    return (jnp.concatenate([cos, cos, ones, cos, cos, ones], axis=1),
            jnp.concatenate([-sin, sin, 0.0 * ones, -sin, sin, 0.0 * ones], axis=1))


def _reorder_odd_weights(w):
    o = np.cumsum([0, ATT_WIDTH, ATT_WIDTH, KV_WIDTH, KV_WIDTH, IDX_HEADS * IDX_DIM, IDX_DIM, IDX_HEADS]).tolist()
    q, z, k, v, qi, ki, wi = [w[:, o[j]:o[j + 1]] for j in range(7)]
    pad = jnp.zeros((w.shape[0], LANES - IDX_DIM - IDX_HEADS), w.dtype)
    return jnp.concatenate([q, k, qi, ki, wi, pad, z, v], axis=1)


def kernel(x, norm_g, e_w_in, e_lam_re, e_lam_im, e_log_step, e_b_re, e_b_im, e_c_re, e_c_im, e_d_skip,
           e_w_glu, e_b_glu, e_conv_w, e_conv_b, e_ln_g, e_ln_b, e_w_out, o_w_in, o_w_out, final_g):
    bsz, seq, dm = x.shape
    assert bsz == NBATCH and dm == D_MODEL and seq % (2 * ATT_TILE) == 0
    assert norm_g.shape[0] == 2 and e_w_in.shape[0] == 1 and o_w_in.shape[0] == 1
    tt = 64
    ksel = min(TOPK_MAX, seq // 4)
    row = lambda a: a.reshape(1, -1).astype(F32)

    ab_re, ab_im, bb_re, bb_im = _s5_prep(e_lam_re[0], e_lam_im[0], e_log_step[0], e_b_re[0], e_b_im[0])
    bc, cc = _s5_block_weights(bb_re, bb_im, e_c_re[0], e_c_im[0])
    a_re = jnp.broadcast_to(ab_re.reshape(1, S5_NSTATE), (NBATCH, S5_NSTATE))
    a_im = jnp.broadcast_to(ab_im.reshape(1, S5_NSTATE), (NBATCH, S5_NSTATE))
    conv_w = jnp.broadcast_to(e_conv_w[0][:, None, :], (CONV_KERNEL, NBATCH, CONV_WIDTH))

    h1 = _layer0(x.reshape(NBATCH, seq * D_MODEL), row(norm_g[0]), e_w_in[0].astype(BF16), a_re, a_im, bc, cc,
                 row(e_d_skip[0]), e_w_glu[0].astype(BF16), row(e_b_glu[0]), conv_w, row(e_conv_b[0]),
                 row(e_ln_g[0]), row(e_ln_b[0]), e_w_out[0].astype(BF16), seq=seq, tt=tt)
    h1 = h1.reshape(NBATCH * seq, D_MODEL)

    cos_t, sin_t = _rope_tables(seq)
    q, k, qi, ki, wi, gate, v = _inproj1(h1, row(norm_g[1]), _reorder_odd_weights(o_w_in[0]).astype(BF16),
                                         cos_t, sin_t, seq=seq, rows=512)
    return _attention(q, qi, wi, gate, h1, k, v, ki, o_w_out[0].astype(BF16), row(final_g), seq=seq, ksel=ksel)
```

```python
import functools
import math

import numpy as np
import jax
import jax.numpy as jnp
from jax import lax
from jax.experimental import pallas as pl
from jax.experimental.pallas import tpu as pltpu

F32 = jnp.float32
BF16 = jnp.bfloat16
I32 = jnp.int32

NBATCH = 8
D_MODEL = 1024
CHUNK = 64
NORM_EPS = 1e-6
S5_WIDTH = 512
S5_GROUP = 16
S5_GROUPS = 32
S5_STATE = 64
S5_NSTATE = S5_GROUPS * S5_STATE
S5_SLABS = 4
CONV_WIDTH = 512
CONV_KERNEL = 31
CONV_HALO = (CONV_KERNEL - 1) * NBATCH
EVEN_IN = 2 * S5_WIDTH + 3 * CONV_WIDTH
N_HEADS = 16
HEAD_DIM = 64
N_KV_HEADS = 4
Q_PER_KV = 4
ATT_WIDTH = 1024
KV_WIDTH = 256
IDX_HEADS = 8
IDX_DIM = 64
TOPK_MAX = 256
ROPE_THETA = 500000.0
ROT_DIM = 16
LANES = 128
ODD_PAD = 3200
ATT_TILE = 256
PV_ROWS = HEAD_DIM + 16
BISECT_STEPS = 16
PEEL_STEPS = 6
SHIFT_MARGIN = 1.02
SHIFT_MAX = 50.0
INT_MIN = np.int32(-2 ** 31)
F32_MAX = float(np.finfo(np.float32).max)
NEG_BIG = -1e38
LOG2E = math.log2(math.e)

VMEM_LIMIT = 56 * 1024 * 1024


def _dot(a, b):
    return jnp.dot(a, b, preferred_element_type=F32)


def _sigmoid(x):
    return 1.0 / (1.0 + jnp.exp(-x))


def _silu(x):
    return x * _sigmoid(x)


def _gelu_tanh(x):
    c = math.sqrt(2.0 / math.pi)
    return 0.5 * x * (1.0 + jnp.tanh(c * (x + 0.044715 * (x * x * x))))


def _s5_prep_kernel(lr_ref, li_ref, ls_ref, br_ref, bi_ref, abr_ref, abi_ref, bbr_ref, bbi_ref):
    lr = jnp.minimum(lr_ref[...], -1e-4)
    li = li_ref[...]
    dt = jnp.exp(ls_ref[...])
    mag = jnp.exp(lr * dt)
    ab_re = mag * jnp.cos(li * dt)
    ab_im = mag * jnp.sin(li * dt)
    den = lr * lr + li * li
    nr = ab_re - 1.0
    ni = ab_im
    k_re = (nr * lr + ni * li) / den
    k_im = (ni * lr - nr * li) / den
    br = br_ref[...]
    bi = bi_ref[...]
    abr_ref[...] = ab_re
    abi_ref[...] = ab_im
    bbr_ref[...] = k_re * br - k_im * bi
    bbi_ref[...] = k_re * bi + k_im * br


def _s5_prep(lam_re, lam_im, log_step, b_re, b_im):
    g, n, c = b_re.shape
    shape2 = (g * n * c // LANES, LANES)
    expand = lambda a: jnp.broadcast_to(a[:, :, None], (g, n, c)).reshape(shape2)
    ls = jnp.broadcast_to(log_step[:, None], (g, n))
    outs = pl.pallas_call(
        _s5_prep_kernel,
        out_shape=[jax.ShapeDtypeStruct(shape2, F32)] * 4,
        name="s5_prep",
    )(expand(lam_re), expand(lam_im), expand(ls), b_re.reshape(shape2), b_im.reshape(shape2))
    ab_re, ab_im, bb_re, bb_im = [o.reshape(g, n, c) for o in outs]
    return ab_re[:, :, 0], ab_im[:, :, 0], bb_re, bb_im


def _s5_block_weights(bb_re, bb_im, c_re, c_im):
    gl = S5_GROUPS // S5_SLABS
    eye = jnp.eye(gl, dtype=F32)

    def in_map(bb):
        t = bb.reshape(S5_SLABS, gl, S5_STATE, S5_GROUP).transpose(0, 1, 3, 2)
        t = t[:, :, :, None, :] * eye[None, :, None, :, None]
        return t.reshape(S5_SLABS, gl * S5_GROUP, gl * S5_STATE)

    def out_map(cc):
        t = cc.reshape(S5_SLABS, gl, S5_GROUP, S5_STATE).transpose(0, 1, 3, 2)
        t = t[:, :, :, None, :] * eye[None, :, None, :, None]
        return t.reshape(S5_SLABS, gl * S5_STATE, gl * S5_GROUP)

    bc = jnp.concatenate([in_map(bb_re), in_map(bb_im)], axis=2).astype(BF16)
    cc = jnp.concatenate([out_map(c_re), out_map(-c_im)], axis=1).astype(BF16)
    return bc, cc


def _layer0_kernel(x_ref, g_ref, win_ref, are_ref, aim_ref, bc_ref, cc_ref, dsk_ref, wglu_ref, bglu_ref,
                   cw_ref, cb_ref, lng_ref, lnb_ref, wout_ref, o_ref,
                   hn_ref, proj_ref, xs_ref, st_ref, hc_ref, gl_ref, ycat_ref, tm_ref, *, tt):
    rows = NBATCH * tt
    half = S5_NSTATE
    slab_w = half // S5_SLABS

    @pl.when(pl.program_id(0) == 0)
    def _():
        st_ref[...] = jnp.zeros_like(st_ref)
        hc_ref[0:CONV_HALO, :] = jnp.zeros((CONV_HALO, CONV_WIDTH), F32)

    gain = g_ref[...]
    for tp in range(tt // 2):
        xa = x_ref[:, (2 * tp) * D_MODEL:(2 * tp + 1) * D_MODEL]
        xb = x_ref[:, (2 * tp + 1) * D_MODEL:(2 * tp + 2) * D_MODEL]
        xx = jnp.concatenate([xa, xb], axis=0)
        ms = jnp.mean(xx * xx, axis=-1, keepdims=True)
        hn_ref[16 * tp:16 * tp + 16, :] = (xx * lax.rsqrt(ms + NORM_EPS) * gain).astype(BF16)

    proj_ref[...] = _dot(hn_ref[...], win_ref[...])

    for m in range(S5_SLABS):
        u_m = proj_ref[:, LANES * m:LANES * (m + 1)].astype(BF16)
        xm = _dot(u_m, bc_ref[m])
        xs_ref[:, slab_w * m:slab_w * (m + 1)] = xm[:, :slab_w]
        xs_ref[:, half + slab_w * m:half + slab_w * (m + 1)] = xm[:, slab_w:]

    cw = 512
    for cc_i in range(half // cw):
        lo = cc_i * cw
        a_re = are_ref[:, lo:lo + cw]
        a_im = aim_ref[:, lo:lo + cw]

        def step(t, carry, lo=lo, a_re=a_re, a_im=a_im):
            s_re, s_im = carry
            r0 = pl.multiple_of(t * NBATCH, NBATCH)
            x_re = xs_ref[pl.ds(r0, NBATCH), lo:lo + cw]
            x_im = xs_ref[pl.ds(r0, NBATCH), half + lo:half + lo + cw]
            n_re = a_re * s_re - a_im * s_im + x_re
            n_im = a_re * s_im + a_im * s_re + x_im
            xs_ref[pl.ds(r0, NBATCH), lo:lo + cw] = n_re
            xs_ref[pl.ds(r0, NBATCH), half + lo:half + lo + cw] = n_im
            return n_re, n_im

        s_re, s_im = lax.fori_loop(0, tt, step, (st_ref[:, lo:lo + cw], st_ref[:, half + lo:half + lo + cw]),
                                   unroll=True)
        st_ref[:, lo:lo + cw] = s_re
        st_ref[:, half + lo:half + lo + cw] = s_im

    for m in range(S5_SLABS):
        s_re = xs_ref[:, slab_w * m:slab_w * (m + 1)].astype(BF16)
        s_im = xs_ref[:, half + slab_w * m:half + slab_w * (m + 1)].astype(BF16)
        y_m = _dot(s_re, cc_ref[m, 0:slab_w, :]) + _dot(s_im, cc_ref[m, slab_w:2 * slab_w, :])
        y_m = y_m + proj_ref[:, LANES * m:LANES * (m + 1)] * dsk_ref[:, LANES * m:LANES * (m + 1)]
        gl_ref[:, LANES * m:LANES * (m + 1)] = _gelu_tanh(y_m)
    gl = gl_ref[...]
    out_a = gl * _sigmoid(_dot(gl.astype(BF16), wglu_ref[...]) + bglu_ref[...])
    ycat_ref[:, 0:S5_WIDTH] = (out_a * _silu(proj_ref[:, S5_WIDTH:2 * S5_WIDTH])).astype(BF16)

    o = 2 * S5_WIDTH
    hc_ref[CONV_HALO:CONV_HALO + rows, :] = (
        proj_ref[:, o:o + CONV_WIDTH] * _sigmoid(proj_ref[:, o + CONV_WIDTH:o + 2 * CONV_WIDTH]))
    rc = 64
    for cl in range(CONV_WIDTH // LANES):
        lanes = slice(LANES * cl, LANES * (cl + 1))
        taps = [cw_ref[k, :, lanes] for k in range(CONV_KERNEL)]

        def conv_chunk(c, _, lanes=lanes, taps=taps):
            r0 = pl.multiple_of(c * rc, rc)
            accs = [None] * (rc // NBATCH)
            for t in range(rc // NBATCH + CONV_KERNEL - 1):
                h = hc_ref[pl.ds(r0 + NBATCH * t, NBATCH), lanes]
                for rr in range(rc // NBATCH):
                    k = t - rr
                    if 0 <= k < CONV_KERNEL:
                        accs[rr] = h * taps[k] if accs[rr] is None else accs[rr] + h * taps[k]
            gl_ref[pl.ds(r0, rc), lanes] = jnp.concatenate(accs, axis=0)
            return 0

        lax.fori_loop(0, rows // rc, conv_chunk, 0)
    hc_ref[0:CONV_HALO, :] = hc_ref[rows:rows + CONV_HALO, :]
    h = gl_ref[...] + cb_ref[...]
    mu = jnp.mean(h, axis=-1, keepdims=True)
    var = jnp.mean(jnp.square(h - mu), axis=-1, keepdims=True)
    hf = (h - mu) * lax.rsqrt(var + NORM_EPS) * lng_ref[...] + lnb_ref[...]
    z_b = proj_ref[:, o + 2 * CONV_WIDTH:o + 3 * CONV_WIDTH]
    ycat_ref[:, S5_WIDTH:S5_WIDTH + CONV_WIDTH] = (_silu(hf) * _silu(z_b)).astype(BF16)

    y = _dot(ycat_ref[...], wout_ref[...])
    for t in range(tt):
        h = y[NBATCH * t:NBATCH * (t + 1), :] + x_ref[:, t * D_MODEL:(t + 1) * D_MODEL]
        for c in range(D_MODEL // LANES):
            tm_ref[c, NBATCH * t:NBATCH * (t + 1), :] = h[:, LANES * c:LANES * (c + 1)]
    for b in range(NBATCH):
        for c in range(D_MODEL // LANES):
            o_ref[b, :, LANES * c:LANES * (c + 1)] = tm_ref[c, pl.ds(b, tt, stride=NBATCH), :]


def _layer0(x2, gain, w_in, a_re, a_im, bc, cc, d_skip, w_glu, b_glu, conv_w, conv_b, ln_g, ln_b, w_out, *, seq, tt):
    rows = NBATCH * tt
    const = lambda shape: pl.BlockSpec(shape, lambda i: (0,) * len(shape))
    return pl.pallas_call(
        functools.partial(_layer0_kernel, tt=tt),
        grid=(seq // tt,),
        in_specs=[
            pl.BlockSpec((NBATCH, tt * D_MODEL), lambda i: (0, i)),
            const((1, D_MODEL)),
            const((D_MODEL, EVEN_IN)),
            const((NBATCH, S5_NSTATE)),
            const((NBATCH, S5_NSTATE)),
            const((S5_SLABS, LANES, 2 * S5_NSTATE // S5_SLABS)),
            const((S5_SLABS, 2 * S5_NSTATE // S5_SLABS, LANES)),
            const((1, S5_WIDTH)),
            const((S5_WIDTH, S5_WIDTH)),
            const((1, S5_WIDTH)),
            const((CONV_KERNEL, NBATCH, CONV_WIDTH)),
            const((1, CONV_WIDTH)),
            const((1, CONV_WIDTH)),
            const((1, CONV_WIDTH)),
            const((S5_WIDTH + CONV_WIDTH, D_MODEL)),
        ],
        out_specs=pl.BlockSpec((NBATCH, tt, D_MODEL), lambda i: (0, i, 0)),
        out_shape=jax.ShapeDtypeStruct((NBATCH, seq, D_MODEL), F32),
        scratch_shapes=[
            pltpu.VMEM((rows, D_MODEL), BF16),
            pltpu.VMEM((rows, EVEN_IN), F32),
            pltpu.VMEM((rows, 2 * S5_NSTATE), F32),
            pltpu.VMEM((NBATCH, 2 * S5_NSTATE), F32),
            pltpu.VMEM((CONV_HALO + rows, CONV_WIDTH), F32),
            pltpu.VMEM((rows, S5_WIDTH), F32),
            pltpu.VMEM((rows, S5_WIDTH + CONV_WIDTH), BF16),
            pltpu.VMEM((D_MODEL // LANES, rows, LANES), F32),
        ],
        compiler_params=pltpu.CompilerParams(dimension_semantics=("arbitrary",), vmem_limit_bytes=VMEM_LIMIT),
        name="layer0",
    )(x2, gain, w_in, a_re, a_im, bc, cc, d_skip, w_glu, b_glu, conv_w, conv_b, ln_g, ln_b, w_out)


def _inproj1_kernel(h_ref, g_ref, w_ref, cos_ref, sin_ref,
                    q_ref, k_ref, qi_ref, ki_ref, wi_ref, gate_ref, v_ref):
    h = h_ref[...]
    ms = jnp.mean(h * h, axis=-1, keepdims=True)
    hn = (h * lax.rsqrt(ms + NORM_EPS) * g_ref[...]).astype(BF16)
    proj = _dot(hn, w_ref[...])
    cos_t = cos_ref[...]
    sin_t = sin_ref[...]
    lane = lax.broadcasted_iota(I32, cos_t.shape, 1)
    first = (lane & (HEAD_DIM - 1)) < (ROT_DIM // 2)
    low = lane < HEAD_DIM

    def slab(c):
        return proj[:, LANES * c:LANES * (c + 1)]

    def rope(xs, ct, st):
        partner = jnp.where(first, pltpu.roll(xs, LANES - ROT_DIM // 2, 1), pltpu.roll(xs, ROT_DIM // 2, 1))
        return xs * ct + partner * st

    q_scale = (HEAD_DIM ** -0.5) * LOG2E
    for c in range(8):
        q_ref[:, LANES * c:LANES * (c + 1)] = (rope(slab(c), cos_t, sin_t) * q_scale).astype(BF16)
    for c in range(2):
        k_ref[:, LANES * c:LANES * (c + 1)] = rope(slab(8 + c), cos_t, sin_t).astype(BF16)
    for c in range(4):
        qi_ref[:, LANES * c:LANES * (c + 1)] = rope(slab(10 + c), cos_t, sin_t).astype(BF16)
    kiw = rope(slab(14), jnp.where(low, cos_t, 1.0), jnp.where(low, sin_t, 0.0))
    ki_ref[...] = kiw.astype(BF16)
    wi_ref[...] = kiw
    for c in range(8):
        gate_ref[:, LANES * c:LANES * (c + 1)] = _silu(slab(15 + c)).astype(BF16)
    for c in range(2):
        v_ref[:, LANES * c:LANES * (c + 1)] = slab(23 + c).astype(BF16)


def _inproj1(h1, gain, w, cos_t, sin_t, *, seq, rows):
    n = seq * NBATCH
    const = lambda shape: pl.BlockSpec(shape, lambda i: (0,) * len(shape))
    rb = lambda w_: pl.BlockSpec((rows, w_), lambda i: (i, 0))
    pos = pl.BlockSpec((rows, LANES), lambda i: (i % (seq // rows), 0))
    return pl.pallas_call(
        _inproj1_kernel,
        grid=(n // rows,),
        in_specs=[rb(D_MODEL), const((1, D_MODEL)), const((D_MODEL, ODD_PAD)), pos, pos],
        out_specs=[rb(ATT_WIDTH), rb(KV_WIDTH), rb(IDX_HEADS * IDX_DIM), rb(LANES), rb(LANES), rb(ATT_WIDTH),
                   rb(KV_WIDTH)],
        out_shape=[
            jax.ShapeDtypeStruct((n, ATT_WIDTH), BF16),
            jax.ShapeDtypeStruct((n, KV_WIDTH), BF16),
            jax.ShapeDtypeStruct((n, IDX_HEADS * IDX_DIM), BF16),
            jax.ShapeDtypeStruct((n, LANES), BF16),
            jax.ShapeDtypeStruct((n, LANES), F32),
            jax.ShapeDtypeStruct((n, ATT_WIDTH), BF16),
            jax.ShapeDtypeStruct((n, KV_WIDTH), BF16),
        ],
        compiler_params=pltpu.CompilerParams(dimension_semantics=("arbitrary",), vmem_limit_bytes=VMEM_LIMIT),
        name="inproj1",
    )(h1, gain, w, cos_t, sin_t)


def _attn_kernel(q_ref, qi_ref, wi_ref, gate_ref, h1_ref, k_ref, v_ref, ki_ref, wout_ref, fg_ref, o_ref,
                 vt_ref, sc_ref, qh_ref, qih_ref, acc_ref, m_ref, jcut_ref, s_ref, mb_ref, thr_ref, cnt_ref,
                 kn_ref, p_ref, *, seq, ksel):
    tq = ATT_TILE
    i = pl.program_id(1)
    nblk = i + 1

    @pl.when(i == 0)
    def _():
        def transpose_v(kb, _):
            r0 = pl.multiple_of(kb * tq, tq)
            vt = v_ref[pl.ds(r0, tq), :].astype(F32).T
            for j in range(N_KV_HEADS):
                vt_ref[kb, j, 0:HEAD_DIM, :] = vt[HEAD_DIM * j:HEAD_DIM * (j + 1), :].astype(BF16)
                vt_ref[kb, j, HEAD_DIM:PV_ROWS, :] = jnp.ones((PV_ROWS - HEAD_DIM, tq), BF16)
            return 0
        lax.fori_loop(0, seq // tq, transpose_v, 0)

        def key_norm(kb, mx):
            r0 = pl.multiple_of(kb * tq, tq)
            kt = k_ref[pl.ds(r0, tq), :].astype(F32).T
            sq = kt * kt
            for j in range(N_KV_HEADS):
                mx = jnp.maximum(mx, jnp.sum(sq[HEAD_DIM * j:HEAD_DIM * (j + 1), :], axis=0, keepdims=True))
            return mx
        kmax = lax.fori_loop(0, seq // tq, key_norm, jnp.zeros((1, tq), F32))
        kn_ref[...] = jnp.zeros((1, tq), F32) + jnp.max(kmax)

    zero_half = jnp.zeros((HEAD_DIM, tq), BF16)
    for c in range(ATT_WIDTH // LANES):
        t = q_ref[:, LANES * c:LANES * (c + 1)].astype(F32).T
        for half in range(2):
            j, g = divmod(2 * c + half, Q_PER_KV)
            lo = (j % 2) * HEAD_DIM
            cols = slice(tq * g, tq * (g + 1))
            qh_ref[j, lo:lo + HEAD_DIM, cols] = t[HEAD_DIM * half:HEAD_DIM * (half + 1), :].astype(BF16)
            qh_ref[j, HEAD_DIM - lo:2 * HEAD_DIM - lo, cols] = zero_half
    for c in range(IDX_HEADS * IDX_DIM // LANES):
        t = qi_ref[:, LANES * c:LANES * (c + 1)].astype(F32).T
        for half in range(2):
            h = 2 * c + half
            qih_ref[0:IDX_DIM, tq * h:tq * (h + 1)] = t[IDX_DIM * half:IDX_DIM * (half + 1), :].astype(BF16)
    qih_ref[IDX_DIM:LANES, :] = jnp.zeros((LANES - IDX_DIM, IDX_HEADS * tq), BF16)
    w8 = wi_ref[...].T[IDX_DIM:IDX_DIM + IDX_HEADS, :] * (IDX_HEADS ** -0.5) * (IDX_DIM ** -0.5)

    row = lax.broadcasted_iota(I32, (tq, tq), 0)
    col = lax.broadcasted_iota(I32, (tq, tq), 1)
    row_chunk = row >> 6
    col_chunk = col >> 6

    def tree(op, x):
        groups = [x[8 * r:8 * (r + 1)] for r in range(x.shape[0] // 8)]
        chains = groups[:4]
        for r in range(4, len(groups)):
            chains[r % 4] = op(chains[r % 4], groups[r])
        return op(op(chains[0], chains[1]), op(chains[2], chains[3]))

    half_heads = IDX_HEADS // 2

    def index_logits(kb):
        ki = ki_ref[pl.ds(pl.multiple_of(kb * tq, tq), tq), :]
        for part in range(2):
            s_ref[0, part] = _dot(ki, qih_ref[:, part * half_heads * tq:(part + 1) * half_heads * tq])

    def score_block(kb, carry):
        smax, smin = carry
        r0 = pl.multiple_of(kb * tq, tq)
        acc = None
        for h in range(IDX_HEADS):
            sh = s_ref[0, h // half_heads, :, tq * (h % half_heads):tq * (h % half_heads + 1)]
            term = jnp.maximum(sh, 0.0) * w8[h:h + 1, :]
            acc = term if acc is None else acc + term
        acc = jnp.where(acc == 0.0, 0.0, acc)
        later = ((kb - i) * (tq // CHUNK) + row_chunk) > col_chunk
        sc_ref[pl.ds(r0, tq), :] = jnp.where(later, -jnp.inf, acc)
        smax = jnp.maximum(smax, tree(jnp.maximum, jnp.where(later, -jnp.inf, acc)))
        smin = jnp.minimum(smin, tree(jnp.minimum, jnp.where(later, jnp.inf, acc)))
        return smax, smin

    def score_step(kb, carry):
        carry = score_block(kb, carry)
        index_logits(kb + 1)
        return carry

    index_logits(0)
    carry = lax.fori_loop(0, nblk - 1, score_step,
                          (jnp.full((8, tq), -jnp.inf, F32), jnp.full((8, tq), jnp.inf, F32)))
    smax, smin = score_block(nblk - 1, carry)
    smax = jnp.max(smax, axis=0, keepdims=True)
    smin = jnp.min(smin, axis=0, keepdims=True)

    @pl.when(nblk % 2 == 1)
    def _():
        sc_ref[pl.ds(pl.multiple_of(nblk * tq, tq), tq), :] = jnp.full((tq, tq), -jnp.inf, F32)

    def lane_pass(elem, op, init):
        def body(kb2, acc):
            r0 = pl.multiple_of(kb2 * 2 * tq, 2 * tq)
            acc = op(acc, tree(op, elem(sc_ref[pl.ds(r0, tq), :], r0)))
            return op(acc, tree(op, elem(sc_ref[pl.ds(r0 + tq, tq), :], r0 + tq)))
        return lax.fori_loop(0, (nblk + 1) // 2, body, init)

    def count(pred):
        acc = lane_pass(lambda blk, r0: jnp.where(pred(blk, r0), 1, 0).astype(I32), jnp.add,
                        jnp.zeros((8, tq), I32))
        return jnp.sum(acc, axis=0, keepdims=True)

    def max_below(bound):
        acc = lane_pass(lambda blk, r0: jnp.where(blk < bound, blk, -jnp.inf), jnp.maximum,
                        jnp.full((8, tq), -jnp.inf, F32))
        return jnp.max(acc, axis=0, keepdims=True)

    def as_threshold(key):
        ks = key ^ INT_MIN
        return pltpu.bitcast(jnp.where(ks >= 0, ks, ks ^ np.int32(0x7FFFFFFF)), F32)

    def next_up(x):
        bits = pltpu.bitcast(x, I32)
        ks = jnp.where(bits >= 0, bits, bits ^ np.int32(0x7FFFFFFF)) + 1
        return pltpu.bitcast(jnp.where(ks >= 0, ks, ks ^ np.int32(0x7FFFFFFF)), F32)

    n_adm = (CHUNK * (tq // CHUNK) * i + CHUNK) + CHUNK * col_chunk[0:1, :]
    few = n_adm <= ksel

    def halve(_, st):
        lo, hi, c_lo, c_hi = st
        t = 0.5 * lo + 0.5 * hi
        inside = (t > lo) & (t < hi)
        t = jnp.where(inside, t, lo)
        c = count(lambda blk, r0: blk >= t)
        up = inside & (c >= ksel)
        dn = inside & (c < ksel)
        return jnp.where(up, t, lo), jnp.where(dn, t, hi), jnp.where(up, c, c_lo), jnp.where(dn, c, c_hi)

    zero = jnp.zeros((1, tq), I32)
    lo, hi, c_lo, c_hi = lax.fori_loop(
        0, BISECT_STEPS, halve, (smin, next_up(jnp.minimum(smax, F32_MAX)), n_adm, zero))

    def unresolved(c_lo, exact):
        return (c_lo > ksel) & (exact == 0) & jnp.logical_not(few)

    def peel_cond(st):
        it, lo, hi, c_lo, c_hi, exact = st
        return (it < PEEL_STEPS) & (jnp.max(jnp.where(unresolved(c_lo, exact), 1, 0)) > 0)

    def peel(st):
        it, lo, hi, c_lo, c_hi, exact = st
        v = max_below(hi)
        c_v = count(lambda blk, r0: blk >= v)
        act = unresolved(c_lo, exact)
        hit = act & (c_v >= ksel)
        miss = act & (c_v < ksel)
        return (it + 1, jnp.where(hit, v, lo), jnp.where(miss, v, hi), jnp.where(hit, c_v, c_lo),
                jnp.where(miss, c_v, c_hi), jnp.where(hit, 1, exact))

    _, lo, hi, c_lo, c_hi, exact = lax.while_loop(peel_cond, peel, (jnp.int32(0), lo, hi, c_lo, c_hi, zero))
    thr_ref[...] = jnp.where(few, -F32_MAX, lo)
    cnt_ref[0:1, :] = c_lo
    cnt_ref[1:2, :] = c_hi

    @pl.when(jnp.max(jnp.where(unresolved(c_lo, exact), 1, 0)) > 0)
    def _():
        def bisect(it, carry):
            t, cnt_t = carry
            cand = t | (jnp.int32(1) << (31 - it))
            cand_f = as_threshold(cand)
            c = count(lambda blk, r0: blk >= cand_f)
            ok = c >= ksel
            return jnp.where(ok, cand, t), jnp.where(ok, c, cnt_t)

        thr_key, cnt_ge = lax.fori_loop(0, 32, bisect, (zero, zero))
        t = as_threshold(thr_key)
        t = jnp.maximum(jnp.where(t != t, -F32_MAX, t), -F32_MAX)
        thr_ref[...] = t
        cnt_ref[0:1, :] = cnt_ge
        cnt_ref[1:2, :] = count(lambda blk, r0: blk > t)

    thr = thr_ref[...]
    cnt_ge = cnt_ref[0:1, :]
    need = ksel - cnt_ref[1:2, :]

    jbits = int(seq).bit_length()
    jcut_ref[...] = jnp.full((1, tq), (1 << jbits) - 1, I32)

    @pl.when(jnp.max(jnp.where(cnt_ge > ksel, 1, 0)) > 0)
    def _():
        def jbisect(it, jc):
            cand = jc | (jnp.int32(1) << (jbits - 1 - it))
            f = count(lambda blk, r0: (blk == thr) & ((r0 + row) < cand))
            return jnp.where(f <= need, cand, jc)
        jcut_ref[...] = lax.fori_loop(0, jbits, jbisect, jnp.zeros((1, tq), I32))

    jcut = jcut_ref[...]

    acc_ref[...] = jnp.zeros(acc_ref.shape, F32)

    def selected(kb):
        r0 = pl.multiple_of(kb * tq, tq)
        blk = sc_ref[pl.ds(r0, tq), :]
        return r0, (blk > thr) | ((blk == thr) & ((r0 + row) < jcut))

    qn2 = None
    for j in range(N_KV_HEADS):
        x = qh_ref[j].astype(F32)
        ssq = jnp.sum(x * x, axis=0, keepdims=True)
        for g in range(Q_PER_KV):
            part = ssq[:, tq * g:tq * (g + 1)]
            qn2 = part if qn2 is None else jnp.maximum(qn2, part)
    bound = SHIFT_MARGIN * jnp.sqrt(qn2 * kn_ref[...])
    fixed_shift_ok = jnp.max(bound) <= SHIFT_MAX

    @pl.when(fixed_shift_ok)
    def _():
        def probabilities(kb):
            r0, sel = selected(kb)
            shift = jnp.where(sel, bound, -jnp.inf)
            shift = jnp.concatenate([shift] * Q_PER_KV, axis=1)
            for j in range(N_KV_HEADS):
                s = _dot(k_ref[pl.ds(r0, tq), LANES * (j // 2):LANES * (j // 2 + 1)], qh_ref[j])
                p_ref[j] = jnp.exp2(s + shift).astype(BF16)

        def accumulate(kb):
            for j in range(N_KV_HEADS):
                acc_ref[j] += _dot(vt_ref[kb, j], p_ref[j])

        def attend(kb, _):
            accumulate(kb)
            probabilities(kb + 1)
            return 0

        probabilities(0)
        lax.fori_loop(0, nblk - 1, attend, 0)
        accumulate(nblk - 1)

    @pl.when(jnp.logical_not(fixed_shift_ok))
    def _():
        _online_softmax_attend(k_ref, vt_ref, qh_ref, acc_ref, m_ref, s_ref, mb_ref, selected, nblk, tq)

    slabs = []
    for pair in range(N_HEADS // 2):
        j, g0 = pair // 2, 2 * (pair % 2)
        inv_l = 1.0 / acc_ref[j, HEAD_DIM:HEAD_DIM + 1, tq * g0:tq * (g0 + 2)]
        o2 = acc_ref[j, 0:HEAD_DIM, tq * g0:tq * (g0 + 2)] * inv_l
        slabs.append(jnp.concatenate([o2[:, 0:tq], o2[:, tq:2 * tq]], axis=0).T)
    att = jnp.concatenate(slabs, axis=1)
    y = _dot((att * gate_ref[...].astype(F32)).astype(BF16), wout_ref[...])
    hh = h1_ref[...] + y
    ms = jnp.mean(hh * hh, axis=-1, keepdims=True)
    o_ref[0] = hh * lax.rsqrt(ms + NORM_EPS) * fg_ref[...]


def _online_softmax_attend(k_ref, vt_ref, qh_ref, acc_ref, m_ref, s_ref, mb_ref, selected, nblk, tq):
    m_ref[...] = jnp.full(m_ref.shape, NEG_BIG, F32)

    def logits(kb, slot):
        r0, sel = selected(kb)
        bias = jnp.where(sel, 0.0, -jnp.inf)
        bias = jnp.concatenate([bias] * Q_PER_KV, axis=1)
        for j in range(N_KV_HEADS):
            s = _dot(k_ref[pl.ds(r0, tq), LANES * (j // 2):LANES * (j // 2 + 1)], qh_ref[j]) + bias
            s_ref[slot, j] = s
            mb_ref[slot, j:j + 1, :] = jnp.max(s, axis=0, keepdims=True)

    def accumulate(kb, slot):
        for j in range(N_KV_HEADS):
            m_old = m_ref[j:j + 1, :]
            m_new = jnp.maximum(m_old, mb_ref[slot, j:j + 1, :])
            m_ref[j:j + 1, :] = m_new
            p = jnp.exp2(s_ref[slot, j] - m_new).astype(BF16)
            acc_ref[j] = acc_ref[j] * jnp.exp2(m_old - m_new) + _dot(vt_ref[kb, j], p)

    def attend_pair(t, _):
        kb = 2 * t
        logits(kb + 1, 1)
        accumulate(kb, 0)
        logits(kb + 2, 0)
        accumulate(kb + 1, 1)
        return 0

    logits(0, 0)
    npair = (nblk - 1) // 2
    lax.fori_loop(0, npair, attend_pair, 0)
    done = 2 * npair

    @pl.when(nblk - done == 1)
    def _():
        accumulate(done, 0)

    @pl.when(nblk - done == 2)
    def _():
        logits(done + 1, 1)
        accumulate(done, 0)
        accumulate(done + 1, 1)


def _attention(q, qi, wi, gate, h1, k, v, ki, w_out, final_g, *, seq, ksel):
    tq = ATT_TILE
    ntile = seq // tq
    tile = lambda w_: pl.BlockSpec((tq, w_), lambda b, i: (b * ntile + i, 0))
    whole = lambda w_: pl.BlockSpec((seq, w_), lambda b, i: (b, 0))
    const = lambda shape: pl.BlockSpec(shape, lambda b, i: (0,) * len(shape))
    return pl.pallas_call(
        functools.partial(_attn_kernel, seq=seq, ksel=ksel),
        grid=(NBATCH, seq // tq),
        in_specs=[tile(ATT_WIDTH), tile(IDX_HEADS * IDX_DIM), tile(LANES), tile(ATT_WIDTH), tile(D_MODEL),
                  whole(KV_WIDTH), whole(KV_WIDTH), whole(LANES),
                  const((ATT_WIDTH, D_MODEL)), const((1, D_MODEL))],
        out_specs=pl.BlockSpec((1, tq, D_MODEL), lambda b, i: (b, i, 0)),
        out_shape=jax.ShapeDtypeStruct((NBATCH, seq, D_MODEL), F32),
        scratch_shapes=[
            pltpu.VMEM((seq // tq, N_KV_HEADS, PV_ROWS, tq), BF16),
            pltpu.VMEM((seq, tq), F32),
            pltpu.VMEM((N_KV_HEADS, LANES, Q_PER_KV * tq), BF16),
            pltpu.VMEM((LANES, IDX_HEADS * tq), BF16),
            pltpu.VMEM((N_KV_HEADS, PV_ROWS, Q_PER_KV * tq), F32),
            pltpu.VMEM((N_KV_HEADS, Q_PER_KV * tq), F32),
            pltpu.VMEM((1, tq), I32),
            pltpu.VMEM((2, N_KV_HEADS, tq, Q_PER_KV * tq), F32),
            pltpu.VMEM((2, N_KV_HEADS, Q_PER_KV * tq), F32),
            pltpu.VMEM((1, tq), F32),
            pltpu.VMEM((2, tq), I32),
            pltpu.VMEM((1, tq), F32),
            pltpu.VMEM((N_KV_HEADS, tq, Q_PER_KV * tq), BF16),
        ],
        compiler_params=pltpu.CompilerParams(dimension_semantics=("arbitrary", "arbitrary"),
                                             vmem_limit_bytes=VMEM_LIMIT),
        name="dsa_attention",
    )(q, qi, wi, gate, h1, k, v, ki, w_out, final_g)


def _rope_tables(seq):
    pos = jnp.arange(seq, dtype=F32)
    inv = ROPE_THETA ** (-jnp.arange(0, ROT_DIM, 2, dtype=F32) / ROT_DIM)
    ang = pos[:, None] * inv[None, :]
    cos, sin = jnp.cos(ang), jnp.sin(ang)
    ones = jnp.ones((seq, HEAD_DIM - ROT_DIM), F32)
    return (jnp.concatenate([cos, cos, ones, cos, cos, ones], axis=1),
            jnp.concatenate([-sin, sin, 0.0 * ones, -sin, sin, 0.0 * ones], axis=1))


def _reorder_odd_weights(w):
    o = np.cumsum([0, ATT_WIDTH, ATT_WIDTH, KV_WIDTH, KV_WIDTH, IDX_HEADS * IDX_DIM, IDX_DIM, IDX_HEADS]).tolist()
    q, z, k, v, qi, ki, wi = [w[:, o[j]:o[j + 1]] for j in range(7)]
    pad = jnp.zeros((w.shape[0], LANES - IDX_DIM - IDX_HEADS), w.dtype)
    return jnp.concatenate([q, k, qi, ki, wi, pad, z, v], axis=1)


def kernel(x, norm_g, e_w_in, e_lam_re, e_lam_im, e_log_step, e_b_re, e_b_im, e_c_re, e_c_im, e_d_skip,
           e_w_glu, e_b_glu, e_conv_w, e_conv_b, e_ln_g, e_ln_b, e_w_out, o_w_in, o_w_out, final_g):
    bsz, seq, dm = x.shape
    assert bsz == NBATCH and dm == D_MODEL and seq % (2 * ATT_TILE) == 0
    assert norm_g.shape[0] == 2 and e_w_in.shape[0] == 1 and o_w_in.shape[0] == 1
    tt = 64
    ksel = min(TOPK_MAX, seq // 4)
    row = lambda a: a.reshape(1, -1).astype(F32)

    ab_re, ab_im, bb_re, bb_im = _s5_prep(e_lam_re[0], e_lam_im[0], e_log_step[0], e_b_re[0], e_b_im[0])
    bc, cc = _s5_block_weights(bb_re, bb_im, e_c_re[0], e_c_im[0])
    a_re = jnp.broadcast_to(ab_re.reshape(1, S5_NSTATE), (NBATCH, S5_NSTATE))
    a_im = jnp.broadcast_to(ab_im.reshape(1, S5_NSTATE), (NBATCH, S5_NSTATE))
    conv_w = jnp.broadcast_to(e_conv_w[0][:, None, :], (CONV_KERNEL, NBATCH, CONV_WIDTH))

    h1 = _layer0(x.reshape(NBATCH, seq * D_MODEL), row(norm_g[0]), e_w_in[0].astype(BF16), a_re, a_im, bc, cc,
                 row(e_d_skip[0]), e_w_glu[0].astype(BF16), row(e_b_glu[0]), conv_w, row(e_conv_b[0]),
                 row(e_ln_g[0]), row(e_ln_b[0]), e_w_out[0].astype(BF16), seq=seq, tt=tt)
    h1 = h1.reshape(NBATCH * seq, D_MODEL)

    cos_t, sin_t = _rope_tables(seq)
    q, k, qi, ki, wi, gate, v = _inproj1(h1, row(norm_g[1]), _reorder_odd_weights(o_w_in[0]).astype(BF16),
                                         cos_t, sin_t, seq=seq, rows=512)
    return _attention(q, qi, wi, gate, h1, k, v, ki, o_w_out[0].astype(BF16), row(final_g), seq=seq, ksel=ksel)
```

```python
import functools
import math

import numpy as np
import jax
import jax.numpy as jnp
from jax import lax
from jax.experimental import pallas as pl
from jax.experimental.pallas import tpu as pltpu

F32 = jnp.float32
BF16 = jnp.bfloat16
I32 = jnp.int32

NBATCH = 8
D_MODEL = 1024
CHUNK = 64
NORM_EPS = 1e-6
S5_WIDTH = 512
S5_GROUP = 16
S5_GROUPS = 32
S5_STATE = 64
S5_NSTATE = S5_GROUPS * S5_STATE
S5_SLABS = 4
CONV_WIDTH = 512
CONV_KERNEL = 31
CONV_HALO = (CONV_KERNEL - 1) * NBATCH
EVEN_IN = 2 * S5_WIDTH + 3 * CONV_WIDTH
N_HEADS = 16
HEAD_DIM = 64
N_KV_HEADS = 4
Q_PER_KV = 4
ATT_WIDTH = 1024
KV_WIDTH = 256
IDX_HEADS = 8
IDX_DIM = 64
TOPK_MAX = 256
ROPE_THETA = 500000.0
ROT_DIM = 16
LANES = 128
ODD_PAD = 3200
ATT_TILE = 256
PV_ROWS = HEAD_DIM + 16
BISECT_STEPS = 16
PEEL_STEPS = 6
SHIFT_MARGIN = 1.02
SHIFT_MAX = 50.0
INT_MIN = np.int32(-2 ** 31)
F32_MAX = float(np.finfo(np.float32).max)
NEG_BIG = -1e38
LOG2E = math.log2(math.e)

VMEM_LIMIT = 56 * 1024 * 1024


def _dot(a, b):
    return jnp.dot(a, b, preferred_element_type=F32)


def _sigmoid(x):
    return 1.0 / (1.0 + jnp.exp(-x))


def _silu(x):
    return x * _sigmoid(x)


def _gelu_tanh(x):
    c = math.sqrt(2.0 / math.pi)
    return 0.5 * x * (1.0 + jnp.tanh(c * (x + 0.044715 * (x * x * x))))


def _s5_prep_kernel(lr_ref, li_ref, ls_ref, br_ref, bi_ref, abr_ref, abi_ref, bbr_ref, bbi_ref):
    lr = jnp.minimum(lr_ref[...], -1e-4)
    li = li_ref[...]
    dt = jnp.exp(ls_ref[...])
    mag = jnp.exp(lr * dt)
    ab_re = mag * jnp.cos(li * dt)
    ab_im = mag * jnp.sin(li * dt)
    den = lr * lr + li * li
    nr = ab_re - 1.0
    ni = ab_im
    k_re = (nr * lr + ni * li) / den
    k_im = (ni * lr - nr * li) / den
    br = br_ref[...]
    bi = bi_ref[...]
    abr_ref[...] = ab_re
    abi_ref[...] = ab_im
    bbr_ref[...] = k_re * br - k_im * bi
    bbi_ref[...] = k_re * bi + k_im * br


def _s5_prep(lam_re, lam_im, log_step, b_re, b_im):
    g, n, c = b_re.shape
    shape2 = (g * n * c // LANES, LANES)
    expand = lambda a: jnp.broadcast_to(a[:, :, None], (g, n, c)).reshape(shape2)
    ls = jnp.broadcast_to(log_step[:, None], (g, n))
    outs = pl.pallas_call(
        _s5_prep_kernel,
        out_shape=[jax.ShapeDtypeStruct(shape2, F32)] * 4,
        name="s5_prep",
    )(expand(lam_re), expand(lam_im), expand(ls), b_re.reshape(shape2), b_im.reshape(shape2))
    ab_re, ab_im, bb_re, bb_im = [o.reshape(g, n, c) for o in outs]
    return ab_re[:, :, 0], ab_im[:, :, 0], bb_re, bb_im


def _s5_block_weights(bb_re, bb_im, c_re, c_im):
    gl = S5_GROUPS // S5_SLABS
    eye = jnp.eye(gl, dtype=F32)

    def in_map(bb):
        t = bb.reshape(S5_SLABS, gl, S5_STATE, S5_GROUP).transpose(0, 1, 3, 2)
        t = t[:, :, :, None, :] * eye[None, :, None, :, None]
        return t.reshape(S5_SLABS, gl * S5_GROUP, gl * S5_STATE)

    def out_map(cc):
        t = cc.reshape(S5_SLABS, gl, S5_GROUP, S5_STATE).transpose(0, 1, 3, 2)
        t = t[:, :, :, None, :] * eye[None, :, None, :, None]
        return t.reshape(S5_SLABS, gl * S5_STATE, gl * S5_GROUP)

    bc = jnp.concatenate([in_map(bb_re), in_map(bb_im)], axis=2).astype(BF16)
    cc = jnp.concatenate([out_map(c_re), out_map(-c_im)], axis=1).astype(BF16)
    return bc, cc


def _layer0_kernel(x_ref, g_ref, win_ref, are_ref, aim_ref, bc_ref, cc_ref, dsk_ref, wglu_ref, bglu_ref,
                   cw_ref, cb_ref, lng_ref, lnb_ref, wout_ref, o_ref,
                   hn_ref, proj_ref, xs_ref, st_ref, hc_ref, gl_ref, ycat_ref, tm_ref, *, tt):
    rows = NBATCH * tt
    half = S5_NSTATE
    slab_w = half // S5_SLABS

    @pl.when(pl.program_id(0) == 0)
    def _():
        st_ref[...] = jnp.zeros_like(st_ref)
        hc_ref[0:CONV_HALO, :] = jnp.zeros((CONV_HALO, CONV_WIDTH), F32)

    gain = g_ref[...]
    for tp in range(tt // 2):
        xa = x_ref[:, (2 * tp) * D_MODEL:(2 * tp + 1) * D_MODEL]
        xb = x_ref[:, (2 * tp + 1) * D_MODEL:(2 * tp + 2) * D_MODEL]
        xx = jnp.concatenate([xa, xb], axis=0)
        ms = jnp.mean(xx * xx, axis=-1, keepdims=True)
        hn_ref[16 * tp:16 * tp + 16, :] = (xx * lax.rsqrt(ms + NORM_EPS) * gain).astype(BF16)

    proj_ref[...] = _dot(hn_ref[...], win_ref[...])

    for m in range(S5_SLABS):
        u_m = proj_ref[:, LANES * m:LANES * (m + 1)].astype(BF16)
        xm = _dot(u_m, bc_ref[m])
        xs_ref[:, slab_w * m:slab_w * (m + 1)] = xm[:, :slab_w]
        xs_ref[:, half + slab_w * m:half + slab_w * (m + 1)] = xm[:, slab_w:]

    cw = 512
    for cc_i in range(half // cw):
        lo = cc_i * cw
        a_re = are_ref[:, lo:lo + cw]
        a_im = aim_ref[:, lo:lo + cw]

        def step(t, carry, lo=lo, a_re=a_re, a_im=a_im):
            s_re, s_im = carry
            r0 = pl.multiple_of(t * NBATCH, NBATCH)
            x_re = xs_ref[pl.ds(r0, NBATCH), lo:lo + cw]
            x_im = xs_ref[pl.ds(r0, NBATCH), half + lo:half + lo + cw]
            n_re = a_re * s_re - a_im * s_im + x_re
            n_im = a_re * s_im + a_im * s_re + x_im
            xs_ref[pl.ds(r0, NBATCH), lo:lo + cw] = n_re
            xs_ref[pl.ds(r0, NBATCH), half + lo:half + lo + cw] = n_im
            return n_re, n_im

        s_re, s_im = lax.fori_loop(0, tt, step, (st_ref[:, lo:lo + cw], st_ref[:, half + lo:half + lo + cw]),
                                   unroll=True)
        st_ref[:, lo:lo + cw] = s_re
        st_ref[:, half + lo:half + lo + cw] = s_im

    for m in range(S5_SLABS):
        s_re = xs_ref[:, slab_w * m:slab_w * (m + 1)].astype(BF16)
        s_im = xs_ref[:, half + slab_w * m:half + slab_w * (m + 1)].astype(BF16)
        y_m = _dot(s_re, cc_ref[m, 0:slab_w, :]) + _dot(s_im, cc_ref[m, slab_w:2 * slab_w, :])
        y_m = y_m + proj_ref[:, LANES * m:LANES * (m + 1)] * dsk_ref[:, LANES * m:LANES * (m + 1)]
        gl_ref[:, LANES * m:LANES * (m + 1)] = _gelu_tanh(y_m)
    gl = gl_ref[...]
    out_a = gl * _sigmoid(_dot(gl.astype(BF16), wglu_ref[...]) + bglu_ref[...])
    ycat_ref[:, 0:S5_WIDTH] = (out_a * _silu(proj_ref[:, S5_WIDTH:2 * S5_WIDTH])).astype(BF16)

    o = 2 * S5_WIDTH
    hc_ref[CONV_HALO:CONV_HALO + rows, :] = (
        proj_ref[:, o:o + CONV_WIDTH] * _sigmoid(proj_ref[:, o + CONV_WIDTH:o + 2 * CONV_WIDTH]))
    rc = 64
    for cl in range(CONV_WIDTH // LANES):
        lanes = slice(LANES * cl, LANES * (cl + 1))
        taps = [cw_ref[k, :, lanes] for k in range(CONV_KERNEL)]

        def conv_chunk(c, _, lanes=lanes, taps=taps):
            r0 = pl.multiple_of(c * rc, rc)
            accs = [None] * (rc // NBATCH)
            for t in range(rc // NBATCH + CONV_KERNEL - 1):
                h = hc_ref[pl.ds(r0 + NBATCH * t, NBATCH), lanes]
                for rr in range(rc // NBATCH):
                    k = t - rr
                    if 0 <= k < CONV_KERNEL:
                        accs[rr] = h * taps[k] if accs[rr] is None else accs[rr] + h * taps[k]
            gl_ref[pl.ds(r0, rc), lanes] = jnp.concatenate(accs, axis=0)
            return 0

        lax.fori_loop(0, rows // rc, conv_chunk, 0)
    hc_ref[0:CONV_HALO, :] = hc_ref[rows:rows + CONV_HALO, :]
    h = gl_ref[...] + cb_ref[...]
    mu = jnp.mean(h, axis=-1, keepdims=True)
    var = jnp.mean(jnp.square(h - mu), axis=-1, keepdims=True)
    hf = (h - mu) * lax.rsqrt(var + NORM_EPS) * lng_ref[...] + lnb_ref[...]
    z_b = proj_ref[:, o + 2 * CONV_WIDTH:o + 3 * CONV_WIDTH]
    ycat_ref[:, S5_WIDTH:S5_WIDTH + CONV_WIDTH] = (_silu(hf) * _silu(z_b)).astype(BF16)

    y = _dot(ycat_ref[...], wout_ref[...])
    for t in range(tt):
        h = y[NBATCH * t:NBATCH * (t + 1), :] + x_ref[:, t * D_MODEL:(t + 1) * D_MODEL]
        for c in range(D_MODEL // LANES):
            tm_ref[c, NBATCH * t:NBATCH * (t + 1), :] = h[:, LANES * c:LANES * (c + 1)]
    for b in range(NBATCH):
        for c in range(D_MODEL // LANES):
            o_ref[b, :, LANES * c:LANES * (c + 1)] = tm_ref[c, pl.ds(b, tt, stride=NBATCH), :]


def _layer0(x2, gain, w_in, a_re, a_im, bc, cc, d_skip, w_glu, b_glu, conv_w, conv_b, ln_g, ln_b, w_out, *, seq, tt):
    rows = NBATCH * tt
    const = lambda shape: pl.BlockSpec(shape, lambda i: (0,) * len(shape))
    return pl.pallas_call(
        functools.partial(_layer0_kernel, tt=tt),
        grid=(seq // tt,),
        in_specs=[
            pl.BlockSpec((NBATCH, tt * D_MODEL), lambda i: (0, i)),
            const((1, D_MODEL)),
            const((D_MODEL, EVEN_IN)),
            const((NBATCH, S5_NSTATE)),
            const((NBATCH, S5_NSTATE)),
            const((S5_SLABS, LANES, 2 * S5_NSTATE // S5_SLABS)),
            const((S5_SLABS, 2 * S5_NSTATE // S5_SLABS, LANES)),
            const((1, S5_WIDTH)),
            const((S5_WIDTH, S5_WIDTH)),
            const((1, S5_WIDTH)),
            const((CONV_KERNEL, NBATCH, CONV_WIDTH)),
            const((1, CONV_WIDTH)),
            const((1, CONV_WIDTH)),
            const((1, CONV_WIDTH)),
            const((S5_WIDTH + CONV_WIDTH, D_MODEL)),
        ],
        out_specs=pl.BlockSpec((NBATCH, tt, D_MODEL), lambda i: (0, i, 0)),
        out_shape=jax.ShapeDtypeStruct((NBATCH, seq, D_MODEL), F32),
        scratch_shapes=[
            pltpu.VMEM((rows, D_MODEL), BF16),
            pltpu.VMEM((rows, EVEN_IN), F32),
            pltpu.VMEM((rows, 2 * S5_NSTATE), F32),
            pltpu.VMEM((NBATCH, 2 * S5_NSTATE), F32),
            pltpu.VMEM((CONV_HALO + rows, CONV_WIDTH), F32),
            pltpu.VMEM((rows, S5_WIDTH), F32),
            pltpu.VMEM((rows, S5_WIDTH + CONV_WIDTH), BF16),
            pltpu.VMEM((D_MODEL // LANES, rows, LANES), F32),
        ],
        compiler_params=pltpu.CompilerParams(dimension_semantics=("arbitrary",), vmem_limit_bytes=VMEM_LIMIT),
        name="layer0",
    )(x2, gain, w_in, a_re, a_im, bc, cc, d_skip, w_glu, b_glu, conv_w, conv_b, ln_g, ln_b, w_out)


def _inproj1_kernel(h_ref, g_ref, w_ref, cos_ref, sin_ref,
                    q_ref, k_ref, qi_ref, ki_ref, wi_ref, gate_ref, v_ref):
    h = h_ref[...]
    ms = jnp.mean(h * h, axis=-1, keepdims=True)
    hn = (h * lax.rsqrt(ms + NORM_EPS) * g_ref[...]).astype(BF16)
    proj = _dot(hn, w_ref[...])
    cos_t = cos_ref[...]
    sin_t = sin_ref[...]
    lane = lax.broadcasted_iota(I32, cos_t.shape, 1)
    first = (lane & (HEAD_DIM - 1)) < (ROT_DIM // 2)
    low = lane < HEAD_DIM

    def slab(c):
        return proj[:, LANES * c:LANES * (c + 1)]

    def rope(xs, ct, st):
        partner = jnp.where(first, pltpu.roll(xs, LANES - ROT_DIM // 2, 1), pltpu.roll(xs, ROT_DIM // 2, 1))
        return xs * ct + partner * st

    q_scale = (HEAD_DIM ** -0.5) * LOG2E
    for c in range(8):
        q_ref[:, LANES * c:LANES * (c + 1)] = (rope(slab(c), cos_t, sin_t) * q_scale).astype(BF16)
    for c in range(2):
        k_ref[:, LANES * c:LANES * (c + 1)] = rope(slab(8 + c), cos_t, sin_t).astype(BF16)
    for c in range(4):
        qi_ref[:, LANES * c:LANES * (c + 1)] = rope(slab(10 + c), cos_t, sin_t).astype(BF16)
    kiw = rope(slab(14), jnp.where(low, cos_t, 1.0), jnp.where(low, sin_t, 0.0))
    ki_ref[...] = kiw.astype(BF16)
    wi_ref[...] = kiw
    for c in range(8):
        gate_ref[:, LANES * c:LANES * (c + 1)] = _silu(slab(15 + c)).astype(BF16)
    for c in range(2):
        v_ref[:, LANES * c:LANES * (c + 1)] = slab(23 + c).astype(BF16)


def _inproj1(h1, gain, w, cos_t, sin_t, *, seq, rows):
    n = seq * NBATCH
    const = lambda shape: pl.BlockSpec(shape, lambda i: (0,) * len(shape))
    rb = lambda w_: pl.BlockSpec((rows, w_), lambda i: (i, 0))
    pos = pl.BlockSpec((rows, LANES), lambda i: (i % (seq // rows), 0))
    return pl.pallas_call(
        _inproj1_kernel,
        grid=(n // rows,),
        in_specs=[rb(D_MODEL), const((1, D_MODEL)), const((D_MODEL, ODD_PAD)), pos, pos],
        out_specs=[rb(ATT_WIDTH), rb(KV_WIDTH), rb(IDX_HEADS * IDX_DIM), rb(LANES), rb(LANES), rb(ATT_WIDTH),
                   rb(KV_WIDTH)],
        out_shape=[
            jax.ShapeDtypeStruct((n, ATT_WIDTH), BF16),
            jax.ShapeDtypeStruct((n, KV_WIDTH), BF16),
            jax.ShapeDtypeStruct((n, IDX_HEADS * IDX_DIM), BF16),
            jax.ShapeDtypeStruct((n, LANES), BF16),
            jax.ShapeDtypeStruct((n, LANES), F32),
            jax.ShapeDtypeStruct((n, ATT_WIDTH), BF16),
            jax.ShapeDtypeStruct((n, KV_WIDTH), BF16),
        ],
        compiler_params=pltpu.CompilerParams(dimension_semantics=("arbitrary",), vmem_limit_bytes=VMEM_LIMIT),
        name="inproj1",
    )(h1, gain, w, cos_t, sin_t)


def _attn_kernel(q_ref, qi_ref, wi_ref, gate_ref, h1_ref, k_ref, v_ref, ki_ref, wout_ref, fg_ref, o_ref,
                 vt_ref, sc_ref, qh_ref, qih_ref, acc_ref, m_ref, jcut_ref, s_ref, mb_ref, thr_ref, cnt_ref,
                 kn_ref, p_ref, *, seq, ksel):
    tq = ATT_TILE
    i = pl.program_id(1)
    nblk = i + 1

    @pl.when(i == 0)
    def _():
        def transpose_v(kb, _):
            r0 = pl.multiple_of(kb * tq, tq)
            vt = v_ref[pl.ds(r0, tq), :].astype(F32).T
            for j in range(N_KV_HEADS):
                vt_ref[kb, j, 0:HEAD_DIM, :] = vt[HEAD_DIM * j:HEAD_DIM * (j + 1), :].astype(BF16)
                vt_ref[kb, j, HEAD_DIM:PV_ROWS, :] = jnp.ones((PV_ROWS - HEAD_DIM, tq), BF16)
            return 0
        lax.fori_loop(0, seq // tq, transpose_v, 0)

        def key_norm(kb, mx):
            r0 = pl.multiple_of(kb * tq, tq)
            kt = k_ref[pl.ds(r0, tq), :].astype(F32).T
            sq = kt * kt
            for j in range(N_KV_HEADS):
                mx = jnp.maximum(mx, jnp.sum(sq[HEAD_DIM * j:HEAD_DIM * (j + 1), :], axis=0, keepdims=True))
            return mx
        kmax = lax.fori_loop(0, seq // tq, key_norm, jnp.zeros((1, tq), F32))
        kn_ref[...] = jnp.zeros((1, tq), F32) + jnp.max(kmax)

    zero_half = jnp.zeros((HEAD_DIM, tq), BF16)
    for c in range(ATT_WIDTH // LANES):
        t = q_ref[:, LANES * c:LANES * (c + 1)].astype(F32).T
        for half in range(2):
            j, g = divmod(2 * c + half, Q_PER_KV)
            lo = (j % 2) * HEAD_DIM
            cols = slice(tq * g, tq * (g + 1))
            qh_ref[j, lo:lo + HEAD_DIM, cols] = t[HEAD_DIM * half:HEAD_DIM * (half + 1), :].astype(BF16)
            qh_ref[j, HEAD_DIM - lo:2 * HEAD_DIM - lo, cols] = zero_half
    for c in range(IDX_HEADS * IDX_DIM // LANES):
        t = qi_ref[:, LANES * c:LANES * (c + 1)].astype(F32).T
        for half in range(2):
            h = 2 * c + half
            qih_ref[0:IDX_DIM, tq * h:tq * (h + 1)] = t[IDX_DIM * half:IDX_DIM * (half + 1), :].astype(BF16)
    qih_ref[IDX_DIM:LANES, :] = jnp.zeros((LANES - IDX_DIM, IDX_HEADS * tq), BF16)
    w8 = wi_ref[...].T[IDX_DIM:IDX_DIM + IDX_HEADS, :] * (IDX_HEADS ** -0.5) * (IDX_DIM ** -0.5)

    row = lax.broadcasted_iota(I32, (tq, tq), 0)
    col = lax.broadcasted_iota(I32, (tq, tq), 1)
    row_chunk = row >> 6
    col_chunk = col >> 6

    def tree(op, x):
        groups = [x[8 * r:8 * (r + 1)] for r in range(x.shape[0] // 8)]
        chains = groups[:4]
        for r in range(4, len(groups)):
            chains[r % 4] = op(chains[r % 4], groups[r])
        return op(op(chains[0], chains[1]), op(chains[2], chains[3]))

    half_heads = IDX_HEADS // 2

    def index_logits(kb):
        ki = ki_ref[pl.ds(pl.multiple_of(kb * tq, tq), tq), :]
        for part in range(2):
            s_ref[0, part] = _dot(ki, qih_ref[:, part * half_heads * tq:(part + 1) * half_heads * tq])

    def score_block(kb, carry, diagonal):
        smax, smin = carry
        r0 = pl.multiple_of(kb * tq, tq)
        acc = None
        for h in range(IDX_HEADS):
            sh = s_ref[0, h // half_heads, :, tq * (h % half_heads):tq * (h % half_heads + 1)]
            term = jnp.maximum(sh, 0.0) * w8[h:h + 1, :]
            acc = term if acc is None else acc + term
        acc = jnp.where(acc == 0.0, 0.0, acc)
        if not diagonal:
            sc_ref[pl.ds(r0, tq), :] = acc
            return jnp.maximum(smax, tree(jnp.maximum, acc)), jnp.minimum(smin, tree(jnp.minimum, acc))
        later = row_chunk > col_chunk
        sc_ref[pl.ds(r0, tq), :] = jnp.where(later, -jnp.inf, acc)
        smax = jnp.maximum(smax, tree(jnp.maximum, jnp.where(later, -jnp.inf, acc)))
        smin = jnp.minimum(smin, tree(jnp.minimum, jnp.where(later, jnp.inf, acc)))
        return smax, smin

    def score_step(kb, carry):
        carry = score_block(kb, carry, diagonal=False)
        index_logits(kb + 1)
        return carry

    index_logits(0)
    carry = lax.fori_loop(0, nblk - 1, score_step,
                          (jnp.full((8, tq), -jnp.inf, F32), jnp.full((8, tq), jnp.inf, F32)))
    smax, smin = score_block(nblk - 1, carry, diagonal=True)
    smax = jnp.max(smax, axis=0, keepdims=True)
    smin = jnp.min(smin, axis=0, keepdims=True)

    @pl.when(nblk % 2 == 1)
    def _():
        sc_ref[pl.ds(pl.multiple_of(nblk * tq, tq), tq), :] = jnp.full((tq, tq), -jnp.inf, F32)

    def lane_pass(elem, op, init):
        def body(kb2, acc):
            r0 = pl.multiple_of(kb2 * 2 * tq, 2 * tq)
            acc = op(acc, tree(op, elem(sc_ref[pl.ds(r0, tq), :], r0)))
            return op(acc, tree(op, elem(sc_ref[pl.ds(r0 + tq, tq), :], r0 + tq)))
        return lax.fori_loop(0, (nblk + 1) // 2, body, init)

    def count(pred):
        acc = lane_pass(lambda blk, r0: jnp.where(pred(blk, r0), 1, 0).astype(I32), jnp.add,
                        jnp.zeros((8, tq), I32))
        return jnp.sum(acc, axis=0, keepdims=True)

    def max_below(bound):
        acc = lane_pass(lambda blk, r0: jnp.where(blk < bound, blk, -jnp.inf), jnp.maximum,
                        jnp.full((8, tq), -jnp.inf, F32))
        return jnp.max(acc, axis=0, keepdims=True)

    def as_threshold(key):
        ks = key ^ INT_MIN
        return pltpu.bitcast(jnp.where(ks >= 0, ks, ks ^ np.int32(0x7FFFFFFF)), F32)

    def next_up(x):
        bits = pltpu.bitcast(x, I32)
        ks = jnp.where(bits >= 0, bits, bits ^ np.int32(0x7FFFFFFF)) + 1
        return pltpu.bitcast(jnp.where(ks >= 0, ks, ks ^ np.int32(0x7FFFFFFF)), F32)

    n_adm = (CHUNK * (tq // CHUNK) * i + CHUNK) + CHUNK * col_chunk[0:1, :]
    few = n_adm <= ksel

    def halve(_, st):
        lo, hi, c_lo, c_hi = st
        t = 0.5 * lo + 0.5 * hi
        inside = (t > lo) & (t < hi)
        t = jnp.where(inside, t, lo)
        c = count(lambda blk, r0: blk >= t)
        up = inside & (c >= ksel)
        dn = inside & (c < ksel)
        return jnp.where(up, t, lo), jnp.where(dn, t, hi), jnp.where(up, c, c_lo), jnp.where(dn, c, c_hi)

    zero = jnp.zeros((1, tq), I32)
    lo, hi, c_lo, c_hi = lax.fori_loop(
        0, BISECT_STEPS, halve, (smin, next_up(jnp.minimum(smax, F32_MAX)), n_adm, zero))

    def unresolved(c_lo, exact):
        return (c_lo > ksel) & (exact == 0) & jnp.logical_not(few)

    def peel_cond(st):
        it, lo, hi, c_lo, c_hi, exact = st
        return (it < PEEL_STEPS) & (jnp.max(jnp.where(unresolved(c_lo, exact), 1, 0)) > 0)

    def peel(st):
        it, lo, hi, c_lo, c_hi, exact = st
        v = max_below(hi)
        c_v = count(lambda blk, r0: blk >= v)
        act = unresolved(c_lo, exact)
        hit = act & (c_v >= ksel)
        miss = act & (c_v < ksel)
        return (it + 1, jnp.where(hit, v, lo), jnp.where(miss, v, hi), jnp.where(hit, c_v, c_lo),
                jnp.where(miss, c_v, c_hi), jnp.where(hit, 1, exact))

    _, lo, hi, c_lo, c_hi, exact = lax.while_loop(peel_cond, peel, (jnp.int32(0), lo, hi, c_lo, c_hi, zero))
    thr_ref[...] = jnp.where(few, -F32_MAX, lo)
    cnt_ref[0:1, :] = c_lo
    cnt_ref[1:2, :] = c_hi

    @pl.when(jnp.max(jnp.where(unresolved(c_lo, exact), 1, 0)) > 0)
    def _():
        def bisect(it, carry):
            t, cnt_t = carry
            cand = t | (jnp.int32(1) << (31 - it))
            cand_f = as_threshold(cand)
            c = count(lambda blk, r0: blk >= cand_f)
            ok = c >= ksel
            return jnp.where(ok, cand, t), jnp.where(ok, c, cnt_t)

        thr_key, cnt_ge = lax.fori_loop(0, 32, bisect, (zero, zero))
        t = as_threshold(thr_key)
        t = jnp.maximum(jnp.where(t != t, -F32_MAX, t), -F32_MAX)
        thr_ref[...] = t
        cnt_ref[0:1, :] = cnt_ge
        cnt_ref[1:2, :] = count(lambda blk, r0: blk > t)

    thr = thr_ref[...]
    cnt_ge = cnt_ref[0:1, :]
    need = ksel - cnt_ref[1:2, :]

    jbits = int(seq).bit_length()
    jcut_ref[...] = jnp.full((1, tq), (1 << jbits) - 1, I32)

    @pl.when(jnp.max(jnp.where(cnt_ge > ksel, 1, 0)) > 0)
    def _():
        def jbisect(it, jc):
            cand = jc | (jnp.int32(1) << (jbits - 1 - it))
            f = count(lambda blk, r0: (blk == thr) & ((r0 + row) < cand))
            return jnp.where(f <= need, cand, jc)
        jcut_ref[...] = lax.fori_loop(0, jbits, jbisect, jnp.zeros((1, tq), I32))

    jcut = jcut_ref[...]

    acc_ref[...] = jnp.zeros(acc_ref.shape, F32)

    def selected(kb):
        r0 = pl.multiple_of(kb * tq, tq)
        blk = sc_ref[pl.ds(r0, tq), :]
        return r0, (blk > thr) | ((blk == thr) & ((r0 + row) < jcut))

    qn2 = None
    for j in range(N_KV_HEADS):
        x = qh_ref[j].astype(F32)
        ssq = jnp.sum(x * x, axis=0, keepdims=True)
        for g in range(Q_PER_KV):
            part = ssq[:, tq * g:tq * (g + 1)]
            qn2 = part if qn2 is None else jnp.maximum(qn2, part)
    bound = SHIFT_MARGIN * jnp.sqrt(qn2 * kn_ref[...])
    fixed_shift_ok = jnp.max(bound) <= SHIFT_MAX

    @pl.when(fixed_shift_ok)
    def _():
        def probabilities(kb):
            r0, sel = selected(kb)
            shift = jnp.where(sel, bound, -jnp.inf)
            shift = jnp.concatenate([shift] * Q_PER_KV, axis=1)
            for j in range(N_KV_HEADS):
                s = _dot(k_ref[pl.ds(r0, tq), LANES * (j // 2):LANES * (j // 2 + 1)], qh_ref[j])
                p_ref[j] = jnp.exp2(s + shift).astype(BF16)

        def accumulate(kb):
            for j in range(N_KV_HEADS):
                acc_ref[j] += _dot(vt_ref[kb, j], p_ref[j])

        def attend(kb, _):
            accumulate(kb)
            probabilities(kb + 1)
            return 0

        probabilities(0)
        lax.fori_loop(0, nblk - 1, attend, 0)
        accumulate(nblk - 1)

    @pl.when(jnp.logical_not(fixed_shift_ok))
    def _():
        _online_softmax_attend(k_ref, vt_ref, qh_ref, acc_ref, m_ref, s_ref, mb_ref, selected, nblk, tq)

    slabs = []
    for pair in range(N_HEADS // 2):
        j, g0 = pair // 2, 2 * (pair % 2)
        inv_l = 1.0 / acc_ref[j, HEAD_DIM:HEAD_DIM + 1, tq * g0:tq * (g0 + 2)]
        o2 = acc_ref[j, 0:HEAD_DIM, tq * g0:tq * (g0 + 2)] * inv_l
        slabs.append(jnp.concatenate([o2[:, 0:tq], o2[:, tq:2 * tq]], axis=0).T)
    att = jnp.concatenate(slabs, axis=1)
    y = _dot((att * gate_ref[...].astype(F32)).astype(BF16), wout_ref[...])
    hh = h1_ref[...] + y
    ms = jnp.mean(hh * hh, axis=-1, keepdims=True)
    o_ref[0] = hh * lax.rsqrt(ms + NORM_EPS) * fg_ref[...]


def _online_softmax_attend(k_ref, vt_ref, qh_ref, acc_ref, m_ref, s_ref, mb_ref, selected, nblk, tq):
    m_ref[...] = jnp.full(m_ref.shape, NEG_BIG, F32)

    def logits(kb, slot):
        r0, sel = selected(kb)
        bias = jnp.where(sel, 0.0, -jnp.inf)
        bias = jnp.concatenate([bias] * Q_PER_KV, axis=1)
        for j in range(N_KV_HEADS):
            s = _dot(k_ref[pl.ds(r0, tq), LANES * (j // 2):LANES * (j // 2 + 1)], qh_ref[j]) + bias
            s_ref[slot, j] = s
            mb_ref[slot, j:j + 1, :] = jnp.max(s, axis=0, keepdims=True)

    def accumulate(kb, slot):
        for j in range(N_KV_HEADS):
            m_old = m_ref[j:j + 1, :]
            m_new = jnp.maximum(m_old, mb_ref[slot, j:j + 1, :])
            m_ref[j:j + 1, :] = m_new
            p = jnp.exp2(s_ref[slot, j] - m_new).astype(BF16)
            acc_ref[j] = acc_ref[j] * jnp.exp2(m_old - m_new) + _dot(vt_ref[kb, j], p)

    def attend_pair(t, _):
        kb = 2 * t
        logits(kb + 1, 1)
        accumulate(kb, 0)
        logits(kb + 2, 0)
        accumulate(kb + 1, 1)
        return 0

    logits(0, 0)
    npair = (nblk - 1) // 2
    lax.fori_loop(0, npair, attend_pair, 0)
    done = 2 * npair

    @pl.when(nblk - done == 1)
    def _():
        accumulate(done, 0)

    @pl.when(nblk - done == 2)
    def _():
        logits(done + 1, 1)
        accumulate(done, 0)
        accumulate(done + 1, 1)


def _attention(q, qi, wi, gate, h1, k, v, ki, w_out, final_g, *, seq, ksel):
    tq = ATT_TILE
    ntile = seq // tq
    tile = lambda w_: pl.BlockSpec((tq, w_), lambda b, i: (b * ntile + i, 0))
    whole = lambda w_: pl.BlockSpec((seq, w_), lambda b, i: (b, 0))
    const = lambda shape: pl.BlockSpec(shape, lambda b, i: (0,) * len(shape))
    return pl.pallas_call(
        functools.partial(_attn_kernel, seq=seq, ksel=ksel),
        grid=(NBATCH, seq // tq),
        in_specs=[tile(ATT_WIDTH), tile(IDX_HEADS * IDX_DIM), tile(LANES), tile(ATT_WIDTH), tile(D_MODEL),
                  whole(KV_WIDTH), whole(KV_WIDTH), whole(LANES),
                  const((ATT_WIDTH, D_MODEL)), const((1, D_MODEL))],
        out_specs=pl.BlockSpec((1, tq, D_MODEL), lambda b, i: (b, i, 0)),
        out_shape=jax.ShapeDtypeStruct((NBATCH, seq, D_MODEL), F32),
        scratch_shapes=[
            pltpu.VMEM((seq // tq, N_KV_HEADS, PV_ROWS, tq), BF16),
            pltpu.VMEM((seq, tq), F32),
            pltpu.VMEM((N_KV_HEADS, LANES, Q_PER_KV * tq), BF16),
            pltpu.VMEM((LANES, IDX_HEADS * tq), BF16),
            pltpu.VMEM((N_KV_HEADS, PV_ROWS, Q_PER_KV * tq), F32),
            pltpu.VMEM((N_KV_HEADS, Q_PER_KV * tq), F32),
            pltpu.VMEM((1, tq), I32),
            pltpu.VMEM((2, N_KV_HEADS, tq, Q_PER_KV * tq), F32),
            pltpu.VMEM((2, N_KV_HEADS, Q_PER_KV * tq), F32),
            pltpu.VMEM((1, tq), F32),
            pltpu.VMEM((2, tq), I32),
            pltpu.VMEM((1, tq), F32),
            pltpu.VMEM((N_KV_HEADS, tq, Q_PER_KV * tq), BF16),
        ],
        compiler_params=pltpu.CompilerParams(dimension_semantics=("arbitrary", "arbitrary"),
                                             vmem_limit_bytes=VMEM_LIMIT),
        name="dsa_attention",
    )(q, qi, wi, gate, h1, k, v, ki, w_out, final_g)


def _rope_tables(seq):
    pos = jnp.arange(seq, dtype=F32)
    inv = ROPE_THETA ** (-jnp.arange(0, ROT_DIM, 2, dtype=F32) / ROT_DIM)
    ang = pos[:, None] * inv[None, :]
    cos, sin = jnp.cos(ang), jnp.sin(ang)
    ones = jnp.ones((seq, HEAD_DIM - ROT_DIM), F32)
    return (jnp.concatenate([cos, cos, ones, cos, cos, ones], axis=1),
            jnp.concatenate([-sin, sin, 0.0 * ones, -sin, sin, 0.0 * ones], axis=1))


def _reorder_odd_weights(w):
    o = np.cumsum([0, ATT_WIDTH, ATT_WIDTH, KV_WIDTH, KV_WIDTH, IDX_HEADS * IDX_DIM, IDX_DIM, IDX_HEADS]).tolist()
    q, z, k, v, qi, ki, wi = [w[:, o[j]:o[j + 1]] for j in range(7)]
    pad = jnp.zeros((w.shape[0], LANES - IDX_DIM - IDX_HEADS), w.dtype)
    return jnp.concatenate([q, k, qi, ki, wi, pad, z, v], axis=1)


def kernel(x, norm_g, e_w_in, e_lam_re, e_lam_im, e_log_step, e_b_re, e_b_im, e_c_re, e_c_im, e_d_skip,
           e_w_glu, e_b_glu, e_conv_w, e_conv_b, e_ln_g, e_ln_b, e_w_out, o_w_in, o_w_out, final_g):
    bsz, seq, dm = x.shape
    assert bsz == NBATCH and dm == D_MODEL and seq % (2 * ATT_TILE) == 0
    assert norm_g.shape[0] == 2 and e_w_in.shape[0] == 1 and o_w_in.shape[0] == 1
    tt = 64
    ksel = min(TOPK_MAX, seq // 4)
    row = lambda a: a.reshape(1, -1).astype(F32)

    ab_re, ab_im, bb_re, bb_im = _s5_prep(e_lam_re[0], e_lam_im[0], e_log_step[0], e_b_re[0], e_b_im[0])
    bc, cc = _s5_block_weights(bb_re, bb_im, e_c_re[0], e_c_im[0])
    a_re = jnp.broadcast_to(ab_re.reshape(1, S5_NSTATE), (NBATCH, S5_NSTATE))
    a_im = jnp.broadcast_to(ab_im.reshape(1, S5_NSTATE), (NBATCH, S5_NSTATE))
    conv_w = jnp.broadcast_to(e_conv_w[0][:, None, :], (CONV_KERNEL, NBATCH, CONV_WIDTH))

    h1 = _layer0(x.reshape(NBATCH, seq * D_MODEL), row(norm_g[0]), e_w_in[0].astype(BF16), a_re, a_im, bc, cc,
                 row(e_d_skip[0]), e_w_glu[0].astype(BF16), row(e_b_glu[0]), conv_w, row(e_conv_b[0]),
                 row(e_ln_g[0]), row(e_ln_b[0]), e_w_out[0].astype(BF16), seq=seq, tt=tt)
    h1 = h1.reshape(NBATCH * seq, D_MODEL)

    cos_t, sin_t = _rope_tables(seq)
    q, k, qi, ki, wi, gate, v = _inproj1(h1, row(norm_g[1]), _reorder_odd_weights(o_w_in[0]).astype(BF16),
                                         cos_t, sin_t, seq=seq, rows=512)
    return _attention(q, qi, wi, gate, h1, k, v, ki, o_w_out[0].astype(BF16), row(final_g), seq=seq, ksel=ksel)
```
